```python
import math
import jax
import jax.numpy as jnp
from jax import lax
import numpy as np

D_MODEL = 2048
BATCH = 4
SEQ = 4096
DEPTH = 2
DEC_BATCH = 128
DEC_SEQ = 1
PAST_LEN = 16384
PAGE_SIZE = 128

N_HEADS = 8
HEAD_DIM = 64
BRANCH_WIDTH = N_HEADS * HEAD_DIM
N_BRANCHES = 4
MLA_Q_LORA = 384
MLA_KV_LORA = 128
MLA_NOPE_DIM = HEAD_DIM
MLA_ROPE_DIM = 32
ROPE_THETA = 10000.0
MOBA_BLOCK = 256
MOBA_TOPK = 3
MOBA_Q_BLOCK = 32
DIFF_QK_DIM = HEAD_DIM // 2
DIFF_V_DIM = HEAD_DIM
FOX_BIAS_INIT = 3.0
N_MEM = 256
MEM_HEADS = 4
MEM_HEAD_DIM = 128
MEM_WIDTH = MEM_HEADS * MEM_HEAD_DIM
N_EXPERTS = 16
N_GROUPS = 4
EXPERTS_PER_GROUP = N_EXPERTS // N_GROUPS
GROUP_SCORE_TOPK = 2
TOP_K = 2
D_EXPERT = 512
Q_BLOCK = 128
ALPHA = (2 * DEPTH) ** 0.25
BETA = (8 * DEPTH) ** -0.25
LN_EPS = 1e-5
RMS_EPS = 1e-6
NEG_INF = -1e30

MLA_COLS = MLA_Q_LORA + MLA_KV_LORA + MLA_ROPE_DIM
MOBA_COLS = BRANCH_WIDTH + 2 * HEAD_DIM
DIFF_COLS = N_HEADS * 2 * DIFF_QK_DIM + 2 * DIFF_QK_DIM + DIFF_V_DIM
FOX_COLS = BRANCH_WIDTH + 2 * HEAD_DIM + N_HEADS
IN_COLS = MLA_COLS + MOBA_COLS + DIFF_COLS + FOX_COLS
MLA_CACHE_W = MLA_KV_LORA + MLA_ROPE_DIM
MOBA_CACHE_W = 2 * HEAD_DIM
DIFF_CACHE_W = 2 * DIFF_QK_DIM + DIFF_V_DIM
FOX_CACHE_W = 2 * HEAD_DIM + N_HEADS

kernel_name = 'hybrid_mla_moba_diff_fox_step'


def _layer_norm(x, g, b):
    xf = x.astype(jnp.float32)
    mu = jnp.mean(xf, -1, keepdims=True)
    var = jnp.mean(jnp.square(xf - mu), -1, keepdims=True)
    return ((xf - mu) * lax.rsqrt(var + LN_EPS) * g + b).astype(x.dtype)


def _rms_norm(x, g):
    xf = x.astype(jnp.float32)
    return (xf * lax.rsqrt(jnp.mean(xf * xf, -1, keepdims=True) + RMS_EPS) * g).astype(x.dtype)


def _rope(x, pos):
    half = x.shape[-1] // 2
    freqs = ROPE_THETA ** (-jnp.arange(half, dtype=jnp.float32) / half)
    ang = pos.astype(jnp.float32)[:, None] * freqs
    ang = ang.reshape(ang.shape[0], *([1] * (x.ndim - 3)), half)
    cos, sin = jnp.cos(ang), jnp.sin(ang)
    x1 = x[..., :half].astype(jnp.float32)
    x2 = x[..., half:].astype(jnp.float32)
    return jnp.concatenate([x1 * cos - x2 * sin, x1 * sin + x2 * cos], -1).astype(x.dtype)


def _alibi_slopes(n):
    return jnp.exp2(-8.0 * jnp.arange(1, n + 1, dtype=jnp.float32) / n)


def _sweep_queries(fn, qpos, qs, bq):
    tq = qpos.shape[0]
    bq = bq if tq % bq == 0 else tq
    nb = tq // bq
    if nb == 1:
        return fn(qpos, *qs)
    pos_b = qpos.reshape(nb, bq)
    qs_b = tuple(jnp.moveaxis(q.reshape(q.shape[0], nb, bq, *q.shape[2:]), 1, 0) for q in qs)
    out = lax.map(lambda a: fn(a[0], *a[1]), (pos_b, qs_b))
    out = jnp.moveaxis(out, 0, 1)
    return out.reshape(out.shape[0], tq, *out.shape[3:])


def _gather_pages(pool, page_table):
    pages = pool[page_table]
    return pages.reshape(pages.shape[0], -1, pages.shape[-1])


def _mla_attend(q_lat, q_rope, qpos, ckv, krope):
    kpos = jnp.arange(ckv.shape[1])
    scale = (MLA_NOPE_DIM + MLA_ROPE_DIM) ** -0.5

    def blk(qp, ql, qr):
        s = jnp.einsum('bqhc,bkc->bhqk', ql, ckv) + jnp.einsum('bqhr,bkr->bhqk', qr, krope)
        s = jnp.where(kpos[None, :] <= qp[:, None], s.astype(jnp.float32) * scale, NEG_INF)
        p = jax.nn.softmax(s, -1).astype(ckv.dtype)
        return jnp.einsum('bhqk,bkc->bqhc', p, ckv)

    return _sweep_queries(blk, qpos, (q_lat, q_rope), Q_BLOCK)


def _moba_attend(q, qpos, k, v, slopes):
    B, L, d = k.shape
    nblk = -(-L // MOBA_BLOCK)
    pad = nblk * MOBA_BLOCK - L
    kb = jnp.pad(k, ((0, 0), (0, pad), (0, 0))).reshape(B, nblk, MOBA_BLOCK, d)
    vb = jnp.pad(v, ((0, 0), (0, pad), (0, 0))).reshape(B, nblk, MOBA_BLOCK, d)
    kmean = jnp.mean(kb.astype(jnp.float32), 2).astype(k.dtype)
    topk = min(MOBA_TOPK, nblk)
    blk_off = jnp.arange(MOBA_BLOCK)
    bidx = jnp.arange(B)[:, None, None, None]
    scale = HEAD_DIM ** -0.5

    def blk(qp, qb):
        bq = qp.shape[0]
        own = qp // MOBA_BLOCK
        gate = jnp.einsum('bqhd,bnd->bqhn', qb, kmean).astype(jnp.float32)
        past = jnp.arange(nblk)[None, :] < own[:, None]
        gate = jnp.where(past[None, :, None, :], gate, NEG_INF)
        _, sel = lax.top_k(gate, topk)
        sel_ok = sel < own[None, :, None, None]
        ks = kb[bidx, sel]
        vs = vb[bidx, sel]
        s_sel = jnp.einsum('bqhd,bqhnkd->bqhnk', qb, ks).astype(jnp.float32) * scale
        dist_sel = (qp[None, :, None, None, None] - (sel[..., None] * MOBA_BLOCK + blk_off)).astype(jnp.float32)
        s_sel = s_sel - slopes[None, None, :, None, None] * dist_sel
        s_sel = jnp.where(sel_ok[..., None], s_sel, NEG_INF)
        ko = kb[:, own]
        vo = vb[:, own]
        s_own = jnp.einsum('bqhd,bqkd->bqhk', qb, ko).astype(jnp.float32) * scale
        dist_own = qp[:, None] - (own[:, None] * MOBA_BLOCK + blk_off)
        s_own = s_own - slopes[None, None, :, None] * dist_own.astype(jnp.float32)[None, :, None, :]
        s_own = jnp.where((dist_own >= 0)[None, :, None, :], s_own, NEG_INF)
        s = jnp.concatenate([s_sel.reshape(B, bq, N_HEADS, topk * MOBA_BLOCK), s_own], -1)
        p = jax.nn.softmax(s, -1).astype(v.dtype)
        p_sel = p[..., :topk * MOBA_BLOCK].reshape(B, bq, N_HEADS, topk, MOBA_BLOCK)
        p_own = p[..., topk * MOBA_BLOCK:]
        return jnp.einsum('bqhnk,bqhnkd->bqhd', p_sel, vs) + jnp.einsum('bqhk,bqkd->bqhd', p_own, vo)

    return _sweep_queries(blk, qpos, (q,), MOBA_Q_BLOCK)


def _diff_attend(q, qpos, k, v, lam, slopes):
    kpos = jnp.arange(k.shape[1])
    scale = DIFF_QK_DIM ** -0.5

    def blk(qp, qb):
        s = jnp.einsum('bqhid,bkid->bihqk', qb, k).astype(jnp.float32) * scale
        dist = (qp[:, None] - kpos[None, :]).astype(jnp.float32)
        s = jnp.where(dist >= 0, s - slopes[:, None, None] * dist, NEG_INF)
        p = jax.nn.softmax(s, -1)
        a = p[:, 0] - lam * p[:, 1]
        return jnp.einsum('bhqk,bkd->bqhd', a.astype(v.dtype), v)

    return _sweep_queries(blk, qpos, (q,), Q_BLOCK)


def _fox_attend(q, qpos, k, v, log_f):
    kpos = jnp.arange(k.shape[1])
    c = lax.cumsum(log_f.astype(jnp.float32), axis=1)
    c_q = c[:, qpos]
    c_k = jnp.swapaxes(c, 1, 2)
    scale = HEAD_DIM ** -0.5

    def blk(qp, qb, cq):
        s = jnp.einsum('bqhd,bkd->bhqk', qb, k).astype(jnp.float32) * scale
        s = s + jnp.swapaxes(cq, 1, 2)[..., None] - c_k[:, :, None, :]
        s = jnp.where(kpos[None, :] <= qp[:, None], s, NEG_INF)
        p = jax.nn.softmax(s, -1).astype(v.dtype)
        return jnp.einsum('bhqk,bkd->bqhd', p, v)

    return _sweep_queries(blk, qpos, (q, c_q), Q_BLOCK)


def _mixer_block(x, pos, past, lp, slopes, lam_init):
    B, T, _ = x.shape
    h = x @ lp['w_in']
    h_a, h_b, h_c, h_d = jnp.split(h, [MLA_COLS, MLA_COLS + MOBA_COLS, MLA_COLS + MOBA_COLS + DIFF_COLS], axis=-1)

    def with_past(rows, i):
        return rows if past is None else jnp.concatenate([past[i], rows], axis=1)

    c_q, c_kv, k_r = jnp.split(h_a, [MLA_Q_LORA, MLA_Q_LORA + MLA_KV_LORA], axis=-1)
    q = jnp.einsum('btc,chd->bthd', _rms_norm(c_q, lp['mla_g_q']), lp['mla_w_q_up'])
    q_lat = jnp.einsum('bthd,chd->bthc', q[..., :MLA_NOPE_DIM], lp['mla_w_uk'])
    q_rope = _rope(q[..., MLA_NOPE_DIM:], pos)
    rows_a = jnp.concatenate([_rms_norm(c_kv, lp['mla_g_kv']), _rope(k_r, pos)], -1)
    keys_a = with_past(rows_a, 0)
    o_lat = _mla_attend(q_lat, q_rope, pos, keys_a[..., :MLA_KV_LORA], keys_a[..., MLA_KV_LORA:])
    o_a = jnp.einsum('bthc,chd->bthd', o_lat, lp['mla_w_uv']).reshape(B, T, BRANCH_WIDTH)

    q_b = h_b[..., :BRANCH_WIDTH].reshape(B, T, N_HEADS, HEAD_DIM)
    rows_b = h_b[..., BRANCH_WIDTH:]
    keys_b = with_past(rows_b, 1)
    o_b = _moba_attend(q_b, pos, keys_b[..., :HEAD_DIM], keys_b[..., HEAD_DIM:], slopes).reshape(B, T, BRANCH_WIDTH)

    nq = N_HEADS * 2 * DIFF_QK_DIM
    q_c = h_c[..., :nq].reshape(B, T, N_HEADS, 2, DIFF_QK_DIM)
    rows_c = h_c[..., nq:]
    keys_c = with_past(rows_c, 2)
    k_c = keys_c[..., :2 * DIFF_QK_DIM].reshape(B, keys_c.shape[1], 2, DIFF_QK_DIM)
    lam_p = lp['diff_lambda'].astype(jnp.float32)
    lam = jnp.exp(jnp.sum(lam_p[0] * lam_p[1])) - jnp.exp(jnp.sum(lam_p[2] * lam_p[3])) + lam_init
    o_c = _diff_attend(q_c, pos, k_c, keys_c[..., 2 * DIFF_QK_DIM:], lam, slopes)
    o_c = (_rms_norm(o_c, lp['diff_g_norm'].reshape(N_HEADS, DIFF_V_DIM)) * (1.0 - lam_init)).reshape(B, T, BRANCH_WIDTH)

    q_d = h_d[..., :BRANCH_WIDTH].reshape(B, T, N_HEADS, HEAD_DIM)
    log_f = jax.nn.log_sigmoid(h_d[..., BRANCH_WIDTH + 2 * HEAD_DIM:].astype(jnp.float32) + lp['fox_b_f']).astype(x.dtype)
    rows_d = jnp.concatenate([h_d[..., BRANCH_WIDTH:BRANCH_WIDTH + 2 * HEAD_DIM], log_f], -1)
    keys_d = with_past(rows_d, 3)
    o_d = _fox_attend(q_d, pos, keys_d[..., :HEAD_DIM], keys_d[..., HEAD_DIM:2 * HEAD_DIM], keys_d[..., 2 * HEAD_DIM:]).reshape(B, T, BRANCH_WIDTH)

    branches = jnp.stack([o_a, o_b, o_c, o_d], axis=2)
    proj = jnp.einsum('btnc,ncd->btnd', branches, lp['w_branch'])
    gates = jax.nn.sigmoid(x @ lp['w_gate'] + lp['b_gate']).reshape(B, T, N_BRANCHES, D_MODEL)
    out = jnp.sum(gates * proj, axis=2) @ lp['w_out']
    return out, (rows_a, rows_b, rows_c, rows_d)


def _mem_cross(x, mem_k, mem_v, w_q, w_o):
    B, T, _ = x.shape
    q = (x @ w_q).reshape(B, T, MEM_HEADS, MEM_HEAD_DIM)
    s = jnp.einsum('bthd,bmhd->bhtm', q, mem_k).astype(jnp.float32) * MEM_HEAD_DIM ** -0.5
    p = jax.nn.softmax(s, -1).astype(mem_v.dtype)
    o = jnp.einsum('bhtm,bmhd->bthd', p, mem_v).reshape(B, T, MEM_WIDTH)
    return o @ w_o


def _moe(x, router_w, router_b, w1, w3, w2):
    B, T, D = x.shape
    xt = x.reshape(B * T, D)
    n = xt.shape[0]
    scores = jax.nn.sigmoid((xt @ router_w).astype(jnp.float32))
    sel_scores = scores + router_b.astype(jnp.float32)
    grp = sel_scores.reshape(n, N_GROUPS, EXPERTS_PER_GROUP)
    grp_score = jnp.sum(lax.top_k(grp, GROUP_SCORE_TOPK)[0], -1)
    g_best = jnp.argmax(grp_score, -1)
    in_grp = (jnp.arange(N_EXPERTS) // EXPERTS_PER_GROUP)[None, :] == g_best[:, None]
    _, top = lax.top_k(jnp.where(in_grp, sel_scores, NEG_INF), TOP_K)
    w_sel = jnp.take_along_axis(scores, top, -1)
    w_sel = w_sel / jnp.sum(w_sel, -1, keepdims=True)
    gate = jnp.zeros_like(scores).at[jnp.arange(n)[:, None], top].set(w_sel)
    hdn = jax.nn.silu(jnp.einsum('nd,edf->nef', xt, w1)) * jnp.einsum('nd,edf->nef', xt, w3)
    hdn = hdn * gate.astype(hdn.dtype)[:, :, None]
    return jnp.einsum('nef,efd->nd', hdn, w2).reshape(B, T, D)


def _trunk_layer(x, pos, past, mem_k, mem_v, lp, router_w, router_b, slopes, lam_init):
    mixed, rows = _mixer_block(x, pos, past, lp, slopes, lam_init)
    x = _layer_norm(ALPHA * x + mixed, lp['ln_g'][0], lp['ln_b'][0])
    x = _layer_norm(ALPHA * x + _mem_cross(x, mem_k, mem_v, lp['mem_w_q'], lp['mem_w_o']), lp['ln_g'][1], lp['ln_b'][1])
    x = _layer_norm(ALPHA * x + _moe(x, router_w, router_b, lp['moe_w1'], lp['moe_w3'], lp['moe_w2']), lp['ln_g'][2], lp['ln_b'][2])
    return x, rows


def setup_inputs(seed: int = 0) -> dict:
    key = jax.random.key(seed)
    keys = jax.random.split(key, 40)

    def nrm(i, shape, scale=1.0):
        return jax.random.normal(keys[i], shape, jnp.float32) * scale

    def gain(i, shape):
        return 1.0 + nrm(i, shape, 0.02)

    n_pages = PAST_LEN // PAGE_SIZE
    n_used = DEC_BATCH * n_pages
    n_pool = n_used + n_used // 4
    page_table = jax.random.permutation(keys[0], n_pool)[:n_used].reshape(DEC_BATCH, n_pages).astype(jnp.int32)
    fox_raw = nrm(6, (DEPTH, n_pool, PAGE_SIZE, FOX_CACHE_W))
    cache_fox = jnp.where(jnp.arange(FOX_CACHE_W) >= 2 * HEAD_DIM, jax.nn.log_sigmoid(FOX_BIAS_INIT + fox_raw), fox_raw)
    return {
        'x_prompt': nrm(1, (BATCH, SEQ, D_MODEL)),
        'x_sample': nrm(2, (DEC_BATCH, DEC_SEQ, D_MODEL)),
        'cache_mla': nrm(3, (DEPTH, n_pool, PAGE_SIZE, MLA_CACHE_W)),
        'cache_moba': nrm(4, (DEPTH, n_pool, PAGE_SIZE, MOBA_CACHE_W)),
        'cache_diff': nrm(5, (DEPTH, n_pool, PAGE_SIZE, DIFF_CACHE_W)),
        'cache_fox': cache_fox,
        'cache_mem_k': nrm(7, (DEPTH, DEC_BATCH, N_MEM, MEM_HEADS, MEM_HEAD_DIM)),
        'cache_mem_v': nrm(8, (DEPTH, DEC_BATCH, N_MEM, MEM_HEADS, MEM_HEAD_DIM)),
        'page_table': page_table,
        'mem_prompt': nrm(9, (BATCH, N_MEM, D_MODEL)),
        'ln_in_g': gain(10, (D_MODEL,)),
        'ln_in_b': nrm(11, (D_MODEL,), 0.02),
        'ln_g': gain(12, (DEPTH, 3, D_MODEL)),
        'ln_b': nrm(13, (DEPTH, 3, D_MODEL), 0.02),
        'w_in': nrm(14, (DEPTH, D_MODEL, IN_COLS), D_MODEL ** -0.5),
        'mla_g_q': gain(15, (DEPTH, MLA_Q_LORA)),
        'mla_g_kv': gain(16, (DEPTH, MLA_KV_LORA)),
        'mla_w_q_up': nrm(17, (DEPTH, MLA_Q_LORA, N_HEADS, MLA_NOPE_DIM + MLA_ROPE_DIM), MLA_Q_LORA ** -0.5),
        'mla_w_uk': nrm(18, (DEPTH, MLA_KV_LORA, N_HEADS, MLA_NOPE_DIM), MLA_KV_LORA ** -0.5),
        'mla_w_uv': nrm(19, (DEPTH, MLA_KV_LORA, N_HEADS, HEAD_DIM), MLA_KV_LORA ** -0.5),
        'diff_lambda': nrm(20, (DEPTH, 4, DIFF_QK_DIM), 0.1),
        'diff_g_norm': gain(21, (DEPTH, BRANCH_WIDTH)),
        'fox_b_f': FOX_BIAS_INIT + nrm(22, (DEPTH, N_HEADS), 0.1),
        'w_branch': nrm(23, (DEPTH, N_BRANCHES, BRANCH_WIDTH, D_MODEL), BRANCH_WIDTH ** -0.5),
        'w_gate': nrm(24, (DEPTH, D_MODEL, N_BRANCHES * D_MODEL), D_MODEL ** -0.5),
        'b_gate': nrm(25, (DEPTH, N_BRANCHES * D_MODEL), 0.02),
        'w_out': nrm(26, (DEPTH, D_MODEL, D_MODEL), BETA * D_MODEL ** -0.5),
        'mem_w_q': nrm(27, (DEPTH, D_MODEL, MEM_WIDTH), D_MODEL ** -0.5),
        'mem_w_k': nrm(28, (DEPTH, D_MODEL, MEM_WIDTH), D_MODEL ** -0.5),
        'mem_w_v': nrm(29, (DEPTH, D_MODEL, MEM_WIDTH), D_MODEL ** -0.5),
        'mem_w_o': nrm(30, (DEPTH, MEM_WIDTH, D_MODEL), BETA * MEM_WIDTH ** -0.5),
        'router_w': nrm(31, (D_MODEL, N_EXPERTS), D_MODEL ** -0.5),
        'router_b': nrm(32, (N_EXPERTS,), 0.01),
        'moe_w1': nrm(33, (DEPTH, N_EXPERTS, D_MODEL, D_EXPERT), D_MODEL ** -0.5),
        'moe_w3': nrm(34, (DEPTH, N_EXPERTS, D_MODEL, D_EXPERT), D_MODEL ** -0.5),
        'moe_w2': nrm(35, (DEPTH, N_EXPERTS, D_EXPERT, D_MODEL), BETA * D_EXPERT ** -0.5),
    }


def reference(x_prompt, x_sample, cache_mla, cache_moba, cache_diff, cache_fox, cache_mem_k, cache_mem_v,
              page_table, mem_prompt, ln_in_g, ln_in_b, ln_g, ln_b, w_in, mla_g_q, mla_g_kv, mla_w_q_up,
              mla_w_uk, mla_w_uv, diff_lambda, diff_g_norm, fox_b_f, w_branch, w_gate, b_gate, w_out,
              mem_w_q, mem_w_k, mem_w_v, mem_w_o, router_w, router_b, moe_w1, moe_w3, moe_w2):
    past_len = page_table.shape[1] * PAGE_SIZE
    pos_p = jnp.arange(x_prompt.shape[1])
    pos_s = past_len + jnp.arange(x_sample.shape[1])
    slopes = _alibi_slopes(N_HEADS)
    xp = _layer_norm(x_prompt, ln_in_g, ln_in_b)
    xs = _layer_norm(x_sample, ln_in_g, ln_in_b)
    rows_p, rows_s, mem_kp, mem_vp = [], [], [], []
    for l in range(DEPTH):
        lam_init = 0.8 - 0.6 * math.exp(-0.3 * l)
        lp = {
            'ln_g': ln_g[l], 'ln_b': ln_b[l], 'w_in': w_in[l],
            'mla_g_q': mla_g_q[l], 'mla_g_kv': mla_g_kv[l], 'mla_w_q_up': mla_w_q_up[l],
            'mla_w_uk': mla_w_uk[l], 'mla_w_uv': mla_w_uv[l],
            'diff_lambda': diff_lambda[l], 'diff_g_norm': diff_g_norm[l], 'fox_b_f': fox_b_f[l],
            'w_branch': w_branch[l], 'w_gate': w_gate[l], 'b_gate': b_gate[l], 'w_out': w_out[l],
            'mem_w_q': mem_w_q[l], 'mem_w_o': mem_w_o[l],
            'moe_w1': moe_w1[l], 'moe_w3': moe_w3[l], 'moe_w2': moe_w2[l],
        }
        bp, nm = mem_prompt.shape[0], mem_prompt.shape[1]
        mk = (mem_prompt @ mem_w_k[l]).reshape(bp, nm, MEM_HEADS, MEM_HEAD_DIM)
        mv = (mem_prompt @ mem_w_v[l]).reshape(bp, nm, MEM_HEADS, MEM_HEAD_DIM)
        xp, rp = _trunk_layer(xp, pos_p, None, mk, mv, lp, router_w, router_b, slopes, lam_init)
        past = (_gather_pages(cache_mla[l], page_table), _gather_pages(cache_moba[l], page_table),
                _gather_pages(cache_diff[l], page_table), _gather_pages(cache_fox[l], page_table))
        xs, rs = _trunk_layer(xs, pos_s, past, cache_mem_k[l], cache_mem_v[l], lp, router_w, router_b, slopes, lam_init)
        rows_p.append(rp)
        rows_s.append(rs)
        mem_kp.append(mk)
        mem_vp.append(mv)
    return (xp, xs,
            jnp.stack([r[0] for r in rows_p]), jnp.stack([r[1] for r in rows_p]),
            jnp.stack([r[2] for r in rows_p]), jnp.stack([r[3] for r in rows_p]),
            jnp.stack(mem_kp), jnp.stack(mem_vp),
            jnp.stack([r[0] for r in rows_s]), jnp.stack([r[1] for r in rows_s]),
            jnp.stack([r[2] for r in rows_s]), jnp.stack([r[3] for r in rows_s]))
```

```python
import functools
import math

import numpy as np
import jax
import jax.numpy as jnp
from jax import lax
from jax.experimental import pallas as pl
from jax.experimental.pallas import tpu as pltpu

F32 = jnp.float32
BF16 = jnp.bfloat16

D_MODEL = 2048
N_HEADS = 8
HEAD_DIM = 64
BRANCH_WIDTH = N_HEADS * HEAD_DIM
N_BRANCHES = 4
MLA_Q_LORA = 384
MLA_KV_LORA = 128
MLA_NOPE_DIM = HEAD_DIM
MLA_ROPE_DIM = 32
ROPE_THETA = 10000.0
MOBA_BLOCK = 256
MOBA_TOPK = 3
DIFF_QK_DIM = HEAD_DIM // 2
DIFF_V_DIM = HEAD_DIM
MEM_HEADS = 4
MEM_HEAD_DIM = 128
MEM_WIDTH = MEM_HEADS * MEM_HEAD_DIM
N_EXPERTS = 16
EXPERTS_PER_GROUP = 4
D_EXPERT = 512
PAGE_SIZE = 128
LN_EPS = 1e-5
RMS_EPS = 1e-6
NEG_INF = -1e30
BELOW_NEG_INF = -3e38

MLA_COLS = MLA_Q_LORA + MLA_KV_LORA + MLA_ROPE_DIM
MOBA_COLS = BRANCH_WIDTH + 2 * HEAD_DIM
DIFF_COLS = N_HEADS * 2 * DIFF_QK_DIM + 2 * DIFF_QK_DIM + DIFF_V_DIM
FOX_COLS = BRANCH_WIDTH + 2 * HEAD_DIM + N_HEADS
IN_COLS = MLA_COLS + MOBA_COLS + DIFF_COLS + FOX_COLS

LANES = 128
VMEM_LIMIT_BYTES = 56 * 1024 * 1024

ATTN_BLOCK = 512
DECODE_PAGES = 16

_NT = (((1,), (1,)), ((), ()))


def _cparams(*sem):
    return pltpu.CompilerParams(dimension_semantics=sem, vmem_limit_bytes=VMEM_LIMIT_BYTES)


def _tile(n, candidates):
    for c in candidates:
        if n % c == 0:
            return c
    return n


def _layer_norm_rows(y, g, b):
    mu = jnp.mean(y, -1, keepdims=True)
    yc = y - mu
    var = jnp.mean(yc * yc, -1, keepdims=True)
    return yc * lax.rsqrt(var + LN_EPS) * g + b


def _ln_body(x_ref, g_ref, b_ref, o_ref, ob_ref):
    y = _layer_norm_rows(x_ref[...], g_ref[...], b_ref[...])
    o_ref[...] = y
    ob_ref[...] = y.astype(BF16)


def _layer_norm(x, g, b):
    m, d = x.shape
    tm = _tile(m, (512, 256, 128))
    return pl.pallas_call(
        _ln_body,
        grid=(m // tm,),
        in_specs=[pl.BlockSpec((tm, d), lambda i: (i, 0)),
                  pl.BlockSpec((1, d), lambda i: (0, 0)),
                  pl.BlockSpec((1, d), lambda i: (0, 0))],
        out_specs=[pl.BlockSpec((tm, d), lambda i: (i, 0)),
                   pl.BlockSpec((tm, d), lambda i: (i, 0))],
        out_shape=[jax.ShapeDtypeStruct((m, d), F32), jax.ShapeDtypeStruct((m, d), BF16)],
        compiler_params=_cparams("parallel"),
        name="layer_norm",
    )(x, g.reshape(1, d), b.reshape(1, d))


def _mm_body(x_ref, w_ref, o_ref):
    o_ref[...] = jnp.dot(x_ref[...].astype(BF16), w_ref[...], preferred_element_type=F32).astype(o_ref.dtype)


def _mm_rms_body(x_ref, g_ref, w_ref, o_ref):
    x = x_ref[...]
    xn = x * lax.rsqrt(jnp.mean(x * x, -1, keepdims=True) + RMS_EPS) * g_ref[...]
    o_ref[...] = jnp.dot(xn.astype(BF16), w_ref[...], preferred_element_type=F32).astype(o_ref.dtype)


def _matmul(x, w, out_dtype=F32, rms_gain=None):
    m, k = x.shape
    n = w.shape[1]
    tm = _tile(m, (512, 256, 128))
    tn = _tile(n, (1024, 768, 640, 512, 384, 256, 128))
    x_spec = pl.BlockSpec((tm, k), lambda i, j: (i, 0))
    w_spec = pl.BlockSpec((k, tn), lambda i, j: (0, j))
    o_spec = pl.BlockSpec((tm, tn), lambda i, j: (i, j))
    if rms_gain is None:
        body, specs, args = _mm_body, [x_spec, w_spec], (x, w)
    else:
        body = _mm_rms_body
        specs = [x_spec, pl.BlockSpec((1, k), lambda i, j: (0, 0)), w_spec]
        args = (x, rms_gain.reshape(1, k), w)
    return pl.pallas_call(
        body, grid=(m // tm, n // tn), in_specs=specs, out_specs=o_spec,
        out_shape=jax.ShapeDtypeStruct((m, n), out_dtype),
        compiler_params=_cparams("parallel", "arbitrary"),
        name="matmul",
    )(*args)


def _bmm_body(x_ref, w_ref, o_ref):
    o_ref[...] = jnp.dot(x_ref[...], w_ref[...], preferred_element_type=F32).astype(o_ref.dtype)


def _head_matmul(x, w, out_dtype):
    h, m, k = x.shape
    n = w.shape[2]
    tm = _tile(m, (1024, 512, 256, 128))
    return pl.pallas_call(
        _bmm_body, grid=(h, m // tm),
        in_specs=[pl.BlockSpec((None, tm, k), lambda a, i: (a, i, 0)),
                  pl.BlockSpec((None, k, n), lambda a, i: (a, 0, 0))],
        out_specs=pl.BlockSpec((None, tm, n), lambda a, i: (a, i, 0)),
        out_shape=jax.ShapeDtypeStruct((h, m, n), out_dtype),
        compiler_params=_cparams("parallel", "arbitrary"),
        name="head_matmul",
    )(x, w)


def _mm_ln_body(x_ref, w_ref, r_ref, g_ref, b_ref, o_ref, ob_ref, *, alpha):
    y = alpha * r_ref[...] + jnp.dot(x_ref[...], w_ref[...], preferred_element_type=F32)
    y = _layer_norm_rows(y, g_ref[...], b_ref[...])
    o_ref[...] = y
    ob_ref[...] = y.astype(BF16)


def _matmul_residual_ln(x, w, resid, g, b, alpha):
    m, k = x.shape
    d = w.shape[1]
    tm = _tile(m, (256, 128))
    row = lambda i: (i, 0)
    fixed = lambda i: (0, 0)
    return pl.pallas_call(
        functools.partial(_mm_ln_body, alpha=alpha),
        grid=(m // tm,),
        in_specs=[pl.BlockSpec((tm, k), row), pl.BlockSpec((k, d), fixed), pl.BlockSpec((tm, d), row),
                  pl.BlockSpec((1, d), fixed), pl.BlockSpec((1, d), fixed)],
        out_specs=[pl.BlockSpec((tm, d), row), pl.BlockSpec((tm, d), row)],
        out_shape=[jax.ShapeDtypeStruct((m, d), F32), jax.ShapeDtypeStruct((m, d), BF16)],
        compiler_params=_cparams("parallel"),
        name="matmul_residual_ln",
    )(x, w, resid, g.reshape(1, d), b.reshape(1, d))


def _merge_body(x_ref, wg_ref, bg_ref, o_ref, wb_ref, out_ref, acc_ref):
    n = pl.program_id(2)

    @pl.when(n == 0)
    def _():
        acc_ref[...] = jnp.zeros_like(acc_ref)

    gate = jax.nn.sigmoid(jnp.dot(x_ref[...], wg_ref[...], preferred_element_type=F32) + bg_ref[...])
    proj = jnp.dot(o_ref[...], wb_ref[...], preferred_element_type=F32)
    acc_ref[...] += gate * proj

    @pl.when(n == N_BRANCHES - 1)
    def _():
        out_ref[...] = acc_ref[...].astype(out_ref.dtype)


def _gated_merge(x, w_gate, b_gate, branches, w_branch):
    m, d = x.shape
    c = branches.shape[2]
    tm = _tile(m, (1024, 512, 256, 128))
    tn = 512
    return pl.pallas_call(
        _merge_body,
        grid=(m // tm, d // tn, N_BRANCHES),
        in_specs=[pl.BlockSpec((tm, d), lambda i, j, n: (i, 0)),
                  pl.BlockSpec((None, d, tn), lambda i, j, n: (n, 0, j)),
                  pl.BlockSpec((None, 1, tn), lambda i, j, n: (n, 0, j)),
                  pl.BlockSpec((None, tm, c), lambda i, j, n: (n, i, 0)),
                  pl.BlockSpec((None, c, tn), lambda i, j, n: (n, 0, j))],
        out_specs=pl.BlockSpec((tm, tn), lambda i, j, n: (i, j)),
        out_shape=jax.ShapeDtypeStruct((m, d), BF16),
        scratch_shapes=[pltpu.VMEM((tm, tn), F32)],
        compiler_params=_cparams("parallel", "arbitrary", "arbitrary"),
        name="gated_merge",
    )(x, w_gate, b_gate, branches, w_branch)


def _mem_attn_body(q_ref, k_ref, v_ref, o_ref):
    scale = MEM_HEAD_DIM ** -0.5
    outs = []
    for h in range(MEM_HEADS):
        sl = slice(h * MEM_HEAD_DIM, (h + 1) * MEM_HEAD_DIM)
        k = k_ref[:, sl].astype(BF16)
        v = v_ref[:, sl].astype(BF16)
        s = lax.dot_general(q_ref[:, sl], k, _NT, preferred_element_type=F32) * scale
        m = jnp.max(s, -1, keepdims=True)
        e = jnp.exp(s - m)
        p = e / jnp.sum(e, -1, keepdims=True)
        outs.append(jnp.dot(p.astype(BF16), v, preferred_element_type=F32))
    o_ref[...] = jnp.concatenate(outs, axis=-1).astype(o_ref.dtype)


def _mem_attention(q, mem_k, mem_v):
    b, t, w = q.shape
    nm = mem_k.shape[1]
    tq = _tile(t, (512, 256, 128))
    return pl.pallas_call(
        _mem_attn_body, grid=(b, t // tq),
        in_specs=[pl.BlockSpec((None, tq, w), lambda a, i: (a, i, 0)),
                  pl.BlockSpec((None, nm, w), lambda a, i: (a, 0, 0)),
                  pl.BlockSpec((None, nm, w), lambda a, i: (a, 0, 0))],
        out_specs=pl.BlockSpec((None, tq, w), lambda a, i: (a, i, 0)),
        out_shape=jax.ShapeDtypeStruct((b, t, w), BF16),
        compiler_params=_cparams("parallel", "arbitrary"),
        name="mem_attention",
    )(q, mem_k, mem_v)


def _router_body(x_ref, w_ref, b_ref, gate_ref):
    logits = jnp.dot(x_ref[...], w_ref[...], preferred_element_type=F32)
    scores = jax.nn.sigmoid(logits)
    sel = scores + b_ref[...]
    lane = lax.broadcasted_iota(jnp.int32, sel.shape, 1)
    in_grp = lane % EXPERTS_PER_GROUP
    grp = lane // EXPERTS_PER_GROUP
    real = lane < N_EXPERTS

    rank = jnp.zeros(sel.shape, jnp.int32)
    for d in range(1, EXPERTS_PER_GROUP):
        lower = pltpu.roll(sel, d, 1)
        upper = pltpu.roll(sel, LANES - d, 1)
        rank += jnp.where((in_grp >= d) & (lower >= sel), 1, 0)
        rank += jnp.where((in_grp + d < EXPERTS_PER_GROUP) & (upper > sel), 1, 0)
    top2 = rank < 2

    v = jnp.where(top2, sel, 0.0)
    gs = v
    for d in range(1, EXPERTS_PER_GROUP):
        gs += jnp.where(in_grp >= d, pltpu.roll(v, d, 1), 0.0)
        gs += jnp.where(in_grp + d < EXPERTS_PER_GROUP, pltpu.roll(v, LANES - d, 1), 0.0)

    n_groups = N_EXPERTS // EXPERTS_PER_GROUP
    beaten = jnp.zeros(sel.shape, jnp.int32)
    for d in range(1, n_groups):
        lower = pltpu.roll(gs, d * EXPERTS_PER_GROUP, 1)
        upper = pltpu.roll(gs, LANES - d * EXPERTS_PER_GROUP, 1)
        beaten += jnp.where((grp >= d) & (lower >= gs), 1, 0)
        beaten += jnp.where((grp + d < n_groups) & (upper > gs), 1, 0)
    chosen = top2 & (beaten == 0) & real

    w_sel = jnp.where(chosen, scores, 0.0)
    gate_ref[...] = w_sel / jnp.sum(w_sel, -1, keepdims=True)


def _router(x, router_w, router_b):
    m, d = x.shape
    tm = _tile(m, (512, 256, 128))
    return pl.pallas_call(
        _router_body, grid=(m // tm,),
        in_specs=[pl.BlockSpec((tm, d), lambda i: (i, 0)),
                  pl.BlockSpec((d, LANES), lambda i: (0, 0)),
                  pl.BlockSpec((1, LANES), lambda i: (0, 0))],
        out_specs=pl.BlockSpec((tm, LANES), lambda i: (i, 0)),
        out_shape=jax.ShapeDtypeStruct((m, LANES), F32),
        compiler_params=_cparams("parallel"),
        name="moe_router",
    )(x, router_w, router_b)


def _moe_body(x_ref, gate_ref, w1_ref, w3_ref, w2_ref, r_ref, g_ref, b_ref, o_ref, ob_ref, acc_ref, *, alpha):
    e = pl.program_id(1)

    @pl.when(e == 0)
    def _():
        acc_ref[...] = jnp.zeros_like(acc_ref)

    x = x_ref[...]
    h1 = jnp.dot(x, w1_ref[...], preferred_element_type=F32)
    h3 = jnp.dot(x, w3_ref[...], preferred_element_type=F32)
    gate = gate_ref[...]
    lane = lax.broadcasted_iota(jnp.int32, gate.shape, 1)
    ge = jnp.sum(jnp.where(lane == e, gate, 0.0), -1, keepdims=True)
    hdn = jax.nn.silu(h1) * h3 * ge
    acc_ref[...] += jnp.dot(hdn.astype(BF16), w2_ref[...], preferred_element_type=F32)

    @pl.when(e == N_EXPERTS - 1)
    def _():
        y = _layer_norm_rows(alpha * r_ref[...] + acc_ref[...], g_ref[...], b_ref[...])
        o_ref[...] = y
        ob_ref[...] = y.astype(BF16)


def _moe_residual_ln(x, gate, w1, w3, w2, resid, g, b, alpha):
    m, d = x.shape
    f = w1.shape[2]
    tm = _tile(m, (512, 256, 128))
    row = lambda i, e: (i, 0)
    fixed = lambda i, e: (0, 0)
    return pl.pallas_call(
        functools.partial(_moe_body, alpha=alpha),
        grid=(m // tm, N_EXPERTS),
        in_specs=[pl.BlockSpec((tm, d), row), pl.BlockSpec((tm, LANES), row),
                  pl.BlockSpec((None, d, f), lambda i, e: (e, 0, 0)),
                  pl.BlockSpec((None, d, f), lambda i, e: (e, 0, 0)),
                  pl.BlockSpec((None, f, d), lambda i, e: (e, 0, 0)),
                  pl.BlockSpec((tm, d), row), pl.BlockSpec((1, d), fixed), pl.BlockSpec((1, d), fixed)],
        out_specs=[pl.BlockSpec((tm, d), row), pl.BlockSpec((tm, d), row)],
        out_shape=[jax.ShapeDtypeStruct((m, d), F32), jax.ShapeDtypeStruct((m, d), BF16)],
        scratch_shapes=[pltpu.VMEM((tm, d), F32)],
        compiler_params=_cparams("parallel", "arbitrary"),
        name="moe_experts",
    )(x, gate, w1, w3, w2, resid, g.reshape(1, d), b.reshape(1, d))


def _lane_prefix_sum(x):
    lane = lax.broadcasted_iota(jnp.int32, x.shape, 1)
    shift = 1
    while shift < LANES:
        x = x + jnp.where(lane >= shift, pltpu.roll(x, shift, 1), 0.0)
        shift *= 2
    return x


def _cumsum_body(*refs, pieces, n_prefetch):
    x_refs, o_ref, carry_ref = refs[n_prefetch:-2], refs[-2], refs[-1]

    @pl.when(pl.program_id(1) == 0)
    def _():
        carry_ref[...] = jnp.zeros_like(carry_ref)

    carry = carry_ref[...]
    for p in range(pieces):
        x = x_refs[p][...] if len(x_refs) > 1 else x_refs[0][:, p * LANES:(p + 1) * LANES]
        c = _lane_prefix_sum(x) + carry
        o_ref[:, p * LANES:(p + 1) * LANES] = c
        carry = c[:, LANES - 1:LANES]
    carry_ref[...] = carry


def _cumsum_rows(x):
    b, r, l = x.shape
    pieces = _tile(l // LANES, (16, 8, 4, 2, 1))
    w = pieces * LANES
    return pl.pallas_call(
        functools.partial(_cumsum_body, pieces=pieces, n_prefetch=0),
        grid=(b, l // w),
        in_specs=[pl.BlockSpec((None, r, w), lambda a, j: (a, 0, j))],
        out_specs=pl.BlockSpec((None, r, w), lambda a, j: (a, 0, j)),
        out_shape=jax.ShapeDtypeStruct((b, r, l), F32),
        scratch_shapes=[pltpu.VMEM((r, 1), F32)],
        compiler_params=_cparams("parallel", "arbitrary"),
        name="cumsum_rows",
    )(x)


def _cumsum_paged(pages_t, layer, page_table_flat, n_seq, n_pages):
    r = pages_t.shape[2]
    pp = DECODE_PAGES
    specs = [pl.BlockSpec((None, None, r, LANES),
                          functools.partial(lambda a, j, pt, p: (layer, pt[a * n_pages + j * pp + p], 0, 0), p=p))
             for p in range(pp)]
    return pl.pallas_call(
        functools.partial(_cumsum_body, pieces=pp, n_prefetch=1),
        grid_spec=pltpu.PrefetchScalarGridSpec(
            num_scalar_prefetch=1, grid=(n_seq, n_pages // pp),
            in_specs=specs,
            out_specs=pl.BlockSpec((None, r, pp * LANES), lambda a, j, pt: (a, 0, j)),
            scratch_shapes=[pltpu.VMEM((r, 1), F32)]),
        out_shape=jax.ShapeDtypeStruct((n_seq, r, n_pages * LANES), F32),
        compiler_params=_cparams("parallel", "arbitrary"),
        name="cumsum_paged",
    )(page_table_flat, *([pages_t] * pp))


def _block_mean_body(k_ref, o_ref):
    o_ref[...] = jnp.sum(k_ref[...], 0, keepdims=True) * (1.0 / MOBA_BLOCK)


def _moba_block_means(k):
    b, l, d = k.shape
    nb = l // MOBA_BLOCK
    out = pl.pallas_call(
        _block_mean_body, grid=(b, nb),
        in_specs=[pl.BlockSpec((None, MOBA_BLOCK, d), lambda a, n: (a, n, 0))],
        out_specs=pl.BlockSpec((None, None, 1, d), lambda a, n: (a, n, 0, 0)),
        out_shape=jax.ShapeDtypeStruct((b, nb, 1, d), F32),
        compiler_params=_cparams("parallel", "arbitrary"),
        name="moba_block_means",
    )(k)
    return out.reshape(b, nb, d)


def _paged_block_mean_body(pt_ref, *refs):
    page_refs, o_ref = refs[:-1], refs[-1]
    per_block = MOBA_BLOCK // PAGE_SIZE
    for n in range(len(page_refs) // per_block):
        tot = jnp.sum(page_refs[n * per_block][...], 0, keepdims=True)
        for t in range(1, per_block):
            tot = tot + jnp.sum(page_refs[n * per_block + t][...], 0, keepdims=True)
        o_ref[n] = tot * (1.0 / MOBA_BLOCK)


def _moba_block_means_paged(cache, layer, page_table_flat, n_seq, n_pages):
    w = cache.shape[3]
    pp = DECODE_PAGES
    bps = pp * PAGE_SIZE // MOBA_BLOCK
    nb = n_pages * PAGE_SIZE // MOBA_BLOCK
    specs = [pl.BlockSpec((None, None, PAGE_SIZE, w),
                          functools.partial(lambda a, j, pt, p: (layer, pt[a * n_pages + j * pp + p], 0, 0), p=p))
             for p in range(pp)]
    out = pl.pallas_call(
        _paged_block_mean_body,
        grid_spec=pltpu.PrefetchScalarGridSpec(
            num_scalar_prefetch=1, grid=(n_seq, n_pages // pp),
            in_specs=specs,
            out_specs=pl.BlockSpec((None, bps, 1, w), lambda a, j, pt: (a, j, 0, 0))),
        out_shape=jax.ShapeDtypeStruct((n_seq, nb, 1, w), F32),
        compiler_params=_cparams("parallel", "arbitrary"),
        name="moba_block_means_paged",
    )(page_table_flat, *([cache] * pp))
    return out.reshape(n_seq, nb, w)


def _top_blocks(gate, limit):
    lane = lax.broadcasted_iota(jnp.int32, gate.shape, 1)
    cur = jnp.where(lane < limit, gate, NEG_INF)
    picked = jnp.zeros(gate.shape, jnp.bool_)
    for _ in range(MOBA_TOPK):
        cur = jnp.where(picked, BELOW_NEG_INF, cur)
        best = jnp.max(cur, -1, keepdims=True)
        first = jnp.min(jnp.where(cur == best, lane, LANES), -1, keepdims=True)
        picked = picked | (lane == first)
    return picked & (lane < limit)


def _flash_body(qi_ref, kj_ref, *refs, mode, n_heads, blk, scale, lam_init):
    q_ref, k_ref, v_ref = refs[:3]
    refs = refs[3:]
    if mode == "fox":
        cq_ref, ck_ref = refs[:2]
        refs = refs[2:]
    if mode == "moba":
        kmean_ref = refs[0]
        refs = refs[1:]
    if mode == "diff":
        lam_ref, gn_ref = refs[:2]
        refs = refs[2:]
    o_ref, m_ref, l_ref, acc_ref = refs[:4]
    aux_ref = refs[4] if len(refs) > 4 else None

    t = pl.program_id(1)
    i = qi_ref[t]
    j = kj_ref[t]
    sub = blk // MOBA_BLOCK

    row = lax.broadcasted_iota(jnp.int32, (blk, blk), 0)
    col = lax.broadcasted_iota(jnp.int32, (blk, blk), 1)

    @pl.when(j == 0)
    def _():
        m_ref[...] = jnp.full_like(m_ref, NEG_INF)
        l_ref[...] = jnp.zeros_like(l_ref)
        acc_ref[...] = jnp.zeros_like(acc_ref)
        if mode == "fox":
            for h in range(n_heads):
                aux_ref[h] = jnp.sum(jnp.where(row == col, cq_ref[h:h + 1, :], 0.0), -1, keepdims=True)
        if mode == "moba":
            own = (i * blk + lax.broadcasted_iota(jnp.int32, (blk, 1), 0)) // MOBA_BLOCK
            km = kmean_ref[...]
            for h in range(n_heads):
                gate = lax.dot_general(q_ref[h], km, _NT, preferred_element_type=F32)
                aux_ref[h] = jnp.where(_top_blocks(gate, own), 1.0, 0.0)

    def tile(diag):
        k = k_ref[...]
        v = v_ref[...]
        if mode in ("moba", "diff"):
            dist = (row - col + (i - j) * blk).astype(F32)
        if diag:
            causal = row >= col
        if mode == "moba":
            lane = lax.broadcasted_iota(jnp.int32, (blk, LANES), 1)
            if diag:
                own_blk = row // MOBA_BLOCK
                key_blk = col // MOBA_BLOCK
        for h in range(n_heads):
            s = lax.dot_general(q_ref[h], k, _NT, preferred_element_type=F32) * scale
            if mode in ("moba", "diff"):
                s = s - (2.0 ** -(h % N_HEADS + 1)) * dist
            if mode == "fox":
                s = s + aux_ref[h] - ck_ref[h:h + 1, :]
            if mode == "moba":
                sel = aux_ref[h]
                picked = None
                for u in range(sub):
                    on = jnp.sum(jnp.where(lane == j * sub + u, sel, 0.0), -1, keepdims=True) > 0.0
                    if sub > 1:
                        on = on & (col // MOBA_BLOCK == u)
                    picked = on if picked is None else picked | on
                if diag:
                    valid = ((key_blk == own_blk) & causal) | ((key_blk < own_blk) & picked)
                else:
                    valid = picked
                s = jnp.where(valid, s, NEG_INF)
            elif diag:
                s = jnp.where(causal, s, NEG_INF)
            m_prev = m_ref[h]
            m_new = jnp.maximum(m_prev, jnp.max(s, -1, keepdims=True))
            alpha = jnp.exp(m_prev - m_new)
            p = jnp.exp(s - m_new)
            l_ref[h] = alpha * l_ref[h] + jnp.sum(p, -1, keepdims=True)
            acc_ref[h] = alpha * acc_ref[h] + jnp.dot(p.astype(BF16), v, preferred_element_type=F32)
            m_ref[h] = m_new

    @pl.when(j < i)
    def _():
        tile(False)

    @pl.when(j == i)
    def _():
        tile(True)
        if mode == "diff":
            lam = lam_ref[0, 0]
            for h in range(N_HEADS):
                a = acc_ref[h] / l_ref[h] - lam * (acc_ref[N_HEADS + h] / l_ref[N_HEADS + h])
                a = a * lax.rsqrt(jnp.mean(a * a, -1, keepdims=True) + RMS_EPS) * gn_ref[h]
                o_ref[h] = (a * (1.0 - lam_init)).astype(o_ref.dtype)
        else:
            for h in range(n_heads):
                o_ref[h] = (acc_ref[h] / l_ref[h]).astype(o_ref.dtype)


def _flash_attention(mode, q, k, v, scale, *, c=None, kmean=None, lam=None, g_norm=None, lam_init=0.0):
    nh, b, t, dk = q.shape
    dv = v.shape[2]
    blk = ATTN_BLOCK
    nq = t // blk
    pairs = [(i, j) for i in range(nq) for j in range(i + 1)]
    qi = jnp.asarray(np.array([p[0] for p in pairs], np.int32))
    kj = jnp.asarray(np.array([p[1] for p in pairs], np.int32))
    h_out = N_HEADS if mode == "diff" else nh

    in_specs = [pl.BlockSpec((nh, None, blk, dk), lambda a, s, qi, kj: (0, a, qi[s], 0)),
                pl.BlockSpec((None, blk, dk), lambda a, s, qi, kj: (a, kj[s], 0)),
                pl.BlockSpec((None, blk, dv), lambda a, s, qi, kj: (a, kj[s], 0))]
    args = [q, k, v]
    scratch = [pltpu.VMEM((nh, blk, 1), F32), pltpu.VMEM((nh, blk, 1), F32), pltpu.VMEM((nh, blk, dv), F32)]
    if mode == "fox":
        in_specs += [pl.BlockSpec((None, nh, blk), lambda a, s, qi, kj: (a, 0, qi[s])),
                     pl.BlockSpec((None, nh, blk), lambda a, s, qi, kj: (a, 0, kj[s]))]
        args += [c, c]
        scratch.append(pltpu.VMEM((nh, blk, 1), F32))
    if mode == "moba":
        in_specs.append(pl.BlockSpec((None, LANES, dk), lambda a, s, qi, kj: (a, 0, 0)))
        args.append(kmean)
        scratch.append(pltpu.VMEM((nh, blk, LANES), F32))
    if mode == "diff":
        in_specs += [pl.BlockSpec(memory_space=pltpu.SMEM),
                     pl.BlockSpec((N_HEADS, 1, dv), lambda a, s, qi, kj: (0, 0, 0))]
        args += [lam.reshape(1, 1), g_norm]

    return pl.pallas_call(
        functools.partial(_flash_body, mode=mode, n_heads=nh, blk=blk, scale=scale, lam_init=lam_init),
        grid_spec=pltpu.PrefetchScalarGridSpec(
            num_scalar_prefetch=2, grid=(b, len(pairs)), in_specs=in_specs,
            out_specs=pl.BlockSpec((h_out, None, blk, dv), lambda a, s, qi, kj: (0, a, qi[s], 0)),
            scratch_shapes=scratch),
        out_shape=jax.ShapeDtypeStruct((h_out, b, t, dv), BF16),
        compiler_params=_cparams("parallel", "arbitrary"),
        name="flash_" + mode,
    )(qi, kj, *args)


def _decode_body(pt_ref, *refs, mode, n_rows, width, scale, past_len, lam_init):
    pp = DECODE_PAGES
    q_ref, new_ref = refs[:2]
    page_refs = refs[2:2 + pp]
    refs = refs[2 + pp:]
    if mode == "fox":
        ck_ref, cq_ref = refs[:2]
        refs = refs[2:]
    if mode == "diff":
        lam_ref, gn_ref, slope_ref = refs[:3]
        refs = refs[3:]
    o_ref, m_ref, l_ref, acc_ref = refs

    j = pl.program_id(1)
    span = pp * PAGE_SIZE

    @pl.when(j == 0)
    def _():
        m_ref[...] = jnp.full_like(m_ref, NEG_INF)
        l_ref[...] = jnp.zeros_like(l_ref)
        acc_ref[...] = jnp.zeros_like(acc_ref)

    q = q_ref[...]
    keys = [page_refs[p][...].astype(BF16) for p in range(pp)]
    s = jnp.concatenate([lax.dot_general(q, kp, _NT, preferred_element_type=F32) for kp in keys], axis=1) * scale
    if mode == "diff":
        dist = (past_len - j * span - lax.broadcasted_iota(jnp.int32, (1, span), 1)).astype(F32)
        s = s - slope_ref[...] * dist
    if mode == "fox":
        s = s + cq_ref[...] - ck_ref[...]
    m_prev = m_ref[...]
    m_new = jnp.maximum(m_prev, jnp.max(s, -1, keepdims=True))
    alpha = jnp.exp(m_prev - m_new)
    p = jnp.exp(s - m_new)
    l_ref[...] = alpha * l_ref[...] + jnp.sum(p, -1, keepdims=True)
    pb = p.astype(BF16)
    acc = alpha * acc_ref[...]
    for u in range(pp):
        acc = acc + jnp.dot(pb[:, u * PAGE_SIZE:(u + 1) * PAGE_SIZE], keys[u][:, :LANES], preferred_element_type=F32)
    acc_ref[...] = acc
    m_ref[...] = m_new

    @pl.when(j == pl.num_programs(1) - 1)
    def _():
        new = new_ref[...].astype(BF16).astype(F32)
        s_new = jnp.sum(q.astype(F32) * new, -1, keepdims=True) * scale
        m_old = m_ref[...]
        m_fin = jnp.maximum(m_old, s_new)
        a_old = jnp.exp(m_old - m_fin)
        p_new = jnp.exp(s_new - m_fin)
        l_fin = a_old * l_ref[...] + p_new
        acc_fin = a_old * acc_ref[...] + p_new.astype(BF16).astype(F32) * new[:, :LANES]
        out = acc_fin / l_fin
        if mode == "diff":
            a = out[:N_HEADS, DIFF_V_DIM:] - lam_ref[0, 0] * out[N_HEADS:, DIFF_V_DIM:]
            a = a * lax.rsqrt(jnp.mean(a * a, -1, keepdims=True) + RMS_EPS) * gn_ref[...]
            o_ref[...] = (a * (1.0 - lam_init)).astype(o_ref.dtype)
        else:
            o_ref[...] = out.astype(o_ref.dtype)


def _decode_attention(mode, q, new_rows, cache, layer, page_table_flat, scale, *, c_keys=None, c_query=None,
                      lam=None, g_norm=None, lam_init=0.0):
    b, n_rows, width = q.shape
    n_pages = page_table_flat.shape[0] // b
    pp = DECODE_PAGES
    past_len = n_pages * PAGE_SIZE
    in_specs = [pl.BlockSpec((None, n_rows, width), lambda a, j, pt: (a, 0, 0)),
                pl.BlockSpec((None, 1, width), lambda a, j, pt: (a, 0, 0))]
    in_specs += [pl.BlockSpec((None, None, PAGE_SIZE, width),
                              functools.partial(lambda a, j, pt, p: (layer, pt[a * n_pages + j * pp + p], 0, 0), p=p))
                 for p in range(pp)]
    args = [q, new_rows] + [cache] * pp
    out_rows, out_w = n_rows, LANES
    if mode == "fox":
        in_specs += [pl.BlockSpec((None, n_rows, pp * PAGE_SIZE), lambda a, j, pt: (a, 0, j)),
                     pl.BlockSpec((None, n_rows, 1), lambda a, j, pt: (a, 0, 0))]
        args += [c_keys, c_query]
    if mode == "diff":
        in_specs += [pl.BlockSpec(memory_space=pltpu.SMEM),
                     pl.BlockSpec((N_HEADS, DIFF_V_DIM), lambda a, j, pt: (0, 0)),
                     pl.BlockSpec((n_rows, 1), lambda a, j, pt: (0, 0))]
        slopes = jnp.asarray(np.array([2.0 ** -(r % N_HEADS + 1) for r in range(n_rows)], np.float32).reshape(n_rows, 1))
        args += [lam.reshape(1, 1), g_norm, slopes]
        out_rows, out_w = N_HEADS, DIFF_V_DIM
    return pl.pallas_call(
        functools.partial(_decode_body, mode=mode, n_rows=n_rows, width=width, scale=scale, past_len=past_len,
                          lam_init=lam_init),
        grid_spec=pltpu.PrefetchScalarGridSpec(
            num_scalar_prefetch=1, grid=(b, n_pages // pp), in_specs=in_specs,
            out_specs=pl.BlockSpec((None, out_rows, out_w), lambda a, j, pt: (a, 0, 0)),
            scratch_shapes=[pltpu.VMEM((n_rows, 1), F32), pltpu.VMEM((n_rows, 1), F32),
                            pltpu.VMEM((n_rows, LANES), F32)]),
        out_shape=jax.ShapeDtypeStruct((b, out_rows, out_w), BF16),
        compiler_params=_cparams("parallel", "arbitrary"),
        name="decode_" + mode,
    )(page_table_flat, *args)


def _moba_pick_body(q_ref, km_ref, o_ref, *, n_blocks):
    gate = lax.dot_general(q_ref[...], km_ref[...].astype(BF16), _NT, preferred_element_type=F32)
    lane = lax.broadcasted_iota(jnp.int32, gate.shape, 1)
    cur = jnp.where(lane < n_blocks, gate, BELOW_NEG_INF)
    out = jnp.zeros(gate.shape, jnp.int32)
    for n in range(MOBA_TOPK):
        best = jnp.max(cur, -1, keepdims=True)
        first = jnp.min(jnp.where(cur == best, lane, LANES), -1, keepdims=True)
        out = jnp.where(lane == n, first, out)
        cur = jnp.where(lane == first, BELOW_NEG_INF, cur)
    o_ref[...] = out


def _moba_pick(q, kmean_padded, n_blocks):
    b, r, w = q.shape
    return pl.pallas_call(
        functools.partial(_moba_pick_body, n_blocks=n_blocks), grid=(b,),
        in_specs=[pl.BlockSpec((None, r, w), lambda a: (a, 0, 0)),
                  pl.BlockSpec((None, LANES, w), lambda a: (a, 0, 0))],
        out_specs=pl.BlockSpec((None, r, LANES), lambda a: (a, 0, 0)),
        out_shape=jax.ShapeDtypeStruct((b, r, LANES), jnp.int32),
        compiler_params=_cparams("parallel"),
        name="moba_pick",
    )(q, kmean_padded)


def _moba_decode_body(pg_ref, blk_ref, q_ref, new_ref, *refs, scale, past_len):
    page_refs, o_ref = refs[:-1], refs[-1]
    a = pl.program_id(0)
    per_block = MOBA_BLOCK // PAGE_SIZE
    per_head = MOBA_TOPK * per_block
    n_sel = N_HEADS * per_head
    q = q_ref[...]
    row = lax.broadcasted_iota(jnp.int32, (N_HEADS, PAGE_SIZE), 0)
    lane = lax.broadcasted_iota(jnp.int32, (N_HEADS, PAGE_SIZE), 1)
    slope = jnp.exp2(-(row + 1).astype(F32))
    keys, scores = [], []
    for u in range(n_sel):
        h, rest = divmod(u, per_head)
        n, half = divmod(rest, per_block)
        kp = page_refs[u][...].astype(BF16)
        keys.append(kp)
        s = lax.dot_general(q, kp, _NT, preferred_element_type=F32) * scale
        first_pos = blk_ref[a * N_HEADS * MOBA_TOPK + h * MOBA_TOPK + n] * MOBA_BLOCK + half * PAGE_SIZE
        dist = (past_len - first_pos - lane).astype(F32)
        scores.append(jnp.where(row == h, s - slope * dist, NEG_INF))
    s = jnp.concatenate(scores, axis=1)
    new = new_ref[...].astype(BF16).astype(F32)
    s_new = jnp.sum(q.astype(F32) * new, -1, keepdims=True) * scale
    m = jnp.maximum(jnp.max(s, -1, keepdims=True), s_new)
    p = jnp.exp(s - m)
    p_new = jnp.exp(s_new - m)
    denom = jnp.sum(p, -1, keepdims=True) + p_new
    pb = p.astype(BF16)
    acc = p_new.astype(BF16).astype(F32) * new
    for u in range(n_sel):
        acc = acc + jnp.dot(pb[:, u * PAGE_SIZE:(u + 1) * PAGE_SIZE], keys[u], preferred_element_type=F32)
    o_ref[...] = (acc / denom).astype(o_ref.dtype)


def _moba_decode_attention(q, new_rows, cache, layer, sel_pages_flat, sel_blocks_flat, scale, past_len):
    b, r, w = q.shape
    n_sel = N_HEADS * MOBA_TOPK * (MOBA_BLOCK // PAGE_SIZE)
    specs = [pl.BlockSpec((None, r, w), lambda a, pg, bk: (a, 0, 0)),
             pl.BlockSpec((None, 1, w), lambda a, pg, bk: (a, 0, 0))]
    specs += [pl.BlockSpec((None, None, PAGE_SIZE, w),
                           functools.partial(lambda a, pg, bk, u: (layer, pg[a * n_sel + u], 0, 0), u=u))
              for u in range(n_sel)]
    return pl.pallas_call(
        functools.partial(_moba_decode_body, scale=scale, past_len=past_len),
        grid_spec=pltpu.PrefetchScalarGridSpec(
            num_scalar_prefetch=2, grid=(b,), in_specs=specs,
            out_specs=pl.BlockSpec((None, r, w), lambda a, pg, bk: (a, 0, 0))),
        out_shape=jax.ShapeDtypeStruct((b, r, w), BF16),
        compiler_params=_cparams("arbitrary"),
        name="decode_moba",
    )(sel_pages_flat, sel_blocks_flat, q, new_rows, *([cache] * n_sel))


def _rope_tables(pos):
    half = MLA_ROPE_DIM // 2
    freqs = ROPE_THETA ** (-jnp.arange(half, dtype=F32) / half)
    ang = pos.astype(F32)[:, None] * freqs
    return jnp.cos(ang), jnp.sin(ang)


def _rope(x, cos, sin):
    half = x.shape[-1] // 2
    shape = (1, cos.shape[0]) + (1,) * (x.ndim - 3) + (half,)
    cos, sin = cos.reshape(shape), sin.reshape(shape)
    x1, x2 = x[..., :half], x[..., half:]
    return jnp.concatenate([x1 * cos - x2 * sin, x1 * sin + x2 * cos], -1)


def _heads_first(x, b, t, n, d):
    return x.reshape(b, t, n, d).transpose(2, 0, 1, 3)


def _prep_weights(l, w_in, mla_w_q_up, mla_w_uk, mla_w_uv, w_branch, w_gate, b_gate, w_out, mem_w_q, mem_w_k,
                  mem_w_v, mem_w_o, moe_w1, moe_w3, moe_w2):
    pad = (-IN_COLS) % LANES
    return dict(
        w_in=jnp.pad(w_in[l], ((0, 0), (0, pad))).astype(BF16),
        w_q_up=mla_w_q_up[l].reshape(MLA_Q_LORA, -1).astype(BF16),
        w_uk=mla_w_uk[l].transpose(1, 2, 0).astype(BF16),
        w_uv=mla_w_uv[l].transpose(1, 0, 2).astype(BF16),
        w_branch=w_branch[l].astype(BF16),
        w_gate=w_gate[l].reshape(D_MODEL, N_BRANCHES, D_MODEL).transpose(1, 0, 2).astype(BF16),
        b_gate=b_gate[l].reshape(N_BRANCHES, 1, D_MODEL),
        w_out=w_out[l].astype(BF16),
        mem_w_q=mem_w_q[l].astype(BF16), mem_w_k=mem_w_k[l].astype(BF16), mem_w_v=mem_w_v[l].astype(BF16),
        mem_w_o=mem_w_o[l].astype(BF16),
        moe_w1=moe_w1[l].astype(BF16), moe_w3=moe_w3[l].astype(BF16), moe_w2=moe_w2[l].astype(BF16),
    )


def _project_inputs(x_bf16, b, t, pos, wl, mla_g_q, mla_g_kv, fox_b_f):
    n = b * t
    h = _matmul(x_bf16, wl["w_in"])
    o = 0
    c_q, c_kv, k_r = h[:, :MLA_Q_LORA], h[:, MLA_Q_LORA:MLA_Q_LORA + MLA_KV_LORA], h[:, MLA_Q_LORA + MLA_KV_LORA:MLA_COLS]
    o += MLA_COLS
    q_b, rows_b = h[:, o:o + BRANCH_WIDTH], h[:, o + BRANCH_WIDTH:o + MOBA_COLS]
    o += MOBA_COLS
    q_c, rows_c = h[:, o:o + BRANCH_WIDTH], h[:, o + BRANCH_WIDTH:o + DIFF_COLS]
    o += DIFF_COLS
    q_d, kv_d, f_d = h[:, o:o + BRANCH_WIDTH], h[:, o + BRANCH_WIDTH:o + BRANCH_WIDTH + 2 * HEAD_DIM], h[:, o + BRANCH_WIDTH + 2 * HEAD_DIM:o + FOX_COLS]

    cos, sin = _rope_tables(pos)
    q = _matmul(c_q, wl["w_q_up"], rms_gain=mla_g_q).reshape(n, N_HEADS, MLA_NOPE_DIM + MLA_ROPE_DIM)
    q_nope = q[..., :MLA_NOPE_DIM].transpose(1, 0, 2).astype(BF16)
    q_lat = _head_matmul(q_nope, wl["w_uk"], BF16)
    q_rope = _rope(q[..., MLA_NOPE_DIM:].reshape(b, t, N_HEADS, MLA_ROPE_DIM), cos, sin)
    q_rope = q_rope.reshape(n, N_HEADS, MLA_ROPE_DIM).transpose(1, 0, 2).astype(BF16)
    q_a = jnp.concatenate([q_lat, q_rope], -1)
    c_kv_n = c_kv * lax.rsqrt(jnp.mean(c_kv * c_kv, -1, keepdims=True) + RMS_EPS) * mla_g_kv
    rows_a = jnp.concatenate([c_kv_n, _rope(k_r.reshape(b, t, MLA_ROPE_DIM), cos, sin).reshape(n, MLA_ROPE_DIM)], -1)
    log_f = jax.nn.log_sigmoid(f_d + fox_b_f)
    rows_d = jnp.concatenate([kv_d, log_f], -1)
    return dict(q_a=q_a, rows_a=rows_a, q_b=q_b, rows_b=rows_b, q_c=q_c, rows_c=rows_c, q_d=q_d, rows_d=rows_d,
                log_f=log_f)


def _diff_lambda(lam_p, lam_init):
    lam_p = lam_p.astype(F32)
    return jnp.exp(jnp.sum(lam_p[0] * lam_p[1])) - jnp.exp(jnp.sum(lam_p[2] * lam_p[3])) + lam_init


def _finish_layer(x, x_bf16, branches, mem_k, mem_v, b, t, wl, ln_g, ln_b, router_w, router_b, alpha):
    merged = _gated_merge(x_bf16, wl["w_gate"], wl["b_gate"], branches, wl["w_branch"])
    x, x_bf16 = _matmul_residual_ln(merged, wl["w_out"], x, ln_g[0], ln_b[0], alpha)
    q_m = _matmul(x_bf16, wl["mem_w_q"], out_dtype=BF16).reshape(b, t, MEM_WIDTH)
    o_m = _mem_attention(q_m, mem_k, mem_v).reshape(b * t, MEM_WIDTH)
    x, x_bf16 = _matmul_residual_ln(o_m, wl["mem_w_o"], x, ln_g[1], ln_b[1], alpha)
    gate = _router(x_bf16, router_w, router_b)
    return _moe_residual_ln(x_bf16, gate, wl["moe_w1"], wl["moe_w3"], wl["moe_w2"], x, ln_g[2], ln_b[2], alpha)


def _prompt_mixers(pr, b, t, wl, lam, g_norm, lam_init):
    n = b * t
    k_a = pr["rows_a"].reshape(b, t, -1).astype(BF16)
    o_lat = _flash_attention("mla", pr["q_a"].reshape(N_HEADS, b, t, -1), k_a, k_a[..., :MLA_KV_LORA],
                             (MLA_NOPE_DIM + MLA_ROPE_DIM) ** -0.5)
    o_a = _head_matmul(o_lat.reshape(N_HEADS, n, MLA_KV_LORA), wl["w_uv"], BF16)
    rows_b = pr["rows_b"].reshape(b, t, -1)
    kmean = _moba_block_means(rows_b[..., :HEAD_DIM])
    kmean = jnp.pad(kmean, ((0, 0), (0, LANES - kmean.shape[1]), (0, 0))).astype(BF16)
    kv_b = rows_b.astype(BF16)
    o_b = _flash_attention("moba", _heads_first(pr["q_b"], b, t, N_HEADS, HEAD_DIM).astype(BF16),
                           kv_b[..., :HEAD_DIM], kv_b[..., HEAD_DIM:], HEAD_DIM ** -0.5, kmean=kmean)
    q_c = pr["q_c"].reshape(b, t, N_HEADS, 2, DIFF_QK_DIM).transpose(3, 2, 0, 1, 4)
    zero = jnp.zeros_like(q_c[0])
    q_c = jnp.concatenate([jnp.concatenate([q_c[0], zero], -1), jnp.concatenate([zero, q_c[1]], -1)], 0).astype(BF16)
    kv_c = pr["rows_c"].reshape(b, t, -1).astype(BF16)
    o_c = _flash_attention("diff", q_c, kv_c[..., :2 * DIFF_QK_DIM], kv_c[..., 2 * DIFF_QK_DIM:], DIFF_QK_DIM ** -0.5,
                           lam=lam, g_norm=g_norm.reshape(N_HEADS, 1, DIFF_V_DIM), lam_init=lam_init)
    c = _cumsum_rows(pr["log_f"].reshape(b, t, N_HEADS).transpose(0, 2, 1))
    kv_d = pr["rows_d"].reshape(b, t, -1).astype(BF16)
    o_d = _flash_attention("fox", _heads_first(pr["q_d"], b, t, N_HEADS, HEAD_DIM).astype(BF16),
                           kv_d[..., :HEAD_DIM], kv_d[..., HEAD_DIM:2 * HEAD_DIM], HEAD_DIM ** -0.5, c=c)
    to_tokens = lambda o: o.reshape(N_HEADS, n, HEAD_DIM).transpose(1, 0, 2).reshape(n, BRANCH_WIDTH)
    return jnp.stack([to_tokens(o_a), to_tokens(o_b), to_tokens(o_c), to_tokens(o_d)])


def _sample_mixers(pr, b, wl, lam, g_norm, lam_init, layer, caches, logf_pages, pt_flat, n_pages):
    cache_mla, cache_moba, cache_diff, cache_fox = caches
    past_len = n_pages * PAGE_SIZE
    per_tok = lambda q: q.transpose(1, 0, 2)
    o_lat = _decode_attention("mla", per_tok(pr["q_a"]), pr["rows_a"].reshape(b, 1, -1), cache_mla, layer, pt_flat,
                              (MLA_NOPE_DIM + MLA_ROPE_DIM) ** -0.5)
    o_a = _head_matmul(o_lat.transpose(1, 0, 2), wl["w_uv"], BF16).transpose(1, 0, 2)
    q_b = pr["q_b"].reshape(b, N_HEADS, HEAD_DIM)
    q_b = jnp.concatenate([q_b, jnp.zeros_like(q_b)], -1).astype(BF16)
    n_blocks = past_len // MOBA_BLOCK
    kmean = _moba_block_means_paged(cache_moba, layer, pt_flat, b, n_pages)
    kmean = jnp.pad(kmean, ((0, 0), (0, LANES - n_blocks), (0, 0)))
    choice = _moba_pick(q_b, kmean, n_blocks)[:, :, :MOBA_TOPK]
    per_block = MOBA_BLOCK // PAGE_SIZE
    page_pos = (choice[..., None] * per_block + jnp.arange(per_block, dtype=jnp.int32)).reshape(b, -1)
    sel_pages = jnp.take_along_axis(pt_flat.reshape(b, n_pages), page_pos, axis=1)
    o_b = _moba_decode_attention(q_b, pr["rows_b"].reshape(b, 1, -1), cache_moba, layer, sel_pages.reshape(-1),
                                 choice.reshape(-1), HEAD_DIM ** -0.5, past_len)[..., HEAD_DIM:]
    q_c = pr["q_c"].reshape(b, N_HEADS, 2, DIFF_QK_DIM).transpose(0, 2, 1, 3)
    z1 = jnp.zeros((b, N_HEADS, DIFF_QK_DIM), F32)
    z2 = jnp.zeros((b, N_HEADS, DIFF_V_DIM), F32)
    q_c = jnp.concatenate([jnp.concatenate([q_c[:, 0], z1, z2], -1), jnp.concatenate([z1, q_c[:, 1], z2], -1)], 1)
    o_c = _decode_attention("diff", q_c.astype(BF16), pr["rows_c"].reshape(b, 1, -1), cache_diff, layer, pt_flat,
                            DIFF_QK_DIM ** -0.5, lam=lam, g_norm=g_norm.reshape(N_HEADS, DIFF_V_DIM),
                            lam_init=lam_init)
    c_keys = _cumsum_paged(logf_pages, layer, pt_flat, b, n_pages)
    c_query = c_keys[:, :, -1:] + pr["log_f"].reshape(b, N_HEADS, 1)
    q_d = pr["q_d"].reshape(b, N_HEADS, HEAD_DIM)
    q_d = jnp.concatenate([q_d, jnp.zeros((b, N_HEADS, HEAD_DIM + N_HEADS), F32)], -1).astype(BF16)
    o_d = _decode_attention("fox", q_d, pr["rows_d"].reshape(b, 1, -1), cache_fox, layer, pt_flat, HEAD_DIM ** -0.5,
                            c_keys=c_keys, c_query=c_query)[..., HEAD_DIM:]
    flat = lambda o: o.reshape(b, BRANCH_WIDTH)
    return jnp.stack([flat(o_a), flat(o_b), flat(o_c), flat(o_d)])


def kernel(x_prompt, x_sample, cache_mla, cache_moba, cache_diff, cache_fox, cache_mem_k, cache_mem_v, page_table, mem_prompt, ln_in_g, ln_in_b, ln_g, ln_b, w_in, mla_g_q, mla_g_kv, mla_w_q_up, mla_w_uk, mla_w_uv, diff_lambda, diff_g_norm, fox_b_f, w_branch, w_gate, b_gate, w_out, mem_w_q, mem_w_k, mem_w_v, mem_w_o, router_w, router_b, moe_w1, moe_w3, moe_w2):
    bp, tp, d = x_prompt.shape
    bs, ts, _ = x_sample.shape
    assert ts == 1 and tp % ATTN_BLOCK == 0 and tp // MOBA_BLOCK >= MOBA_TOPK
    depth = w_in.shape[0]
    n_pages = page_table.shape[1]
    assert n_pages % DECODE_PAGES == 0 and (n_pages * PAGE_SIZE) % MOBA_BLOCK == 0
    past_len = n_pages * PAGE_SIZE
    alpha = (2 * depth) ** 0.25
    pos_p = jnp.arange(tp)
    pos_s = past_len + jnp.arange(ts)
    pt_flat = page_table.reshape(-1)
    n_mem = mem_prompt.shape[1]

    router_w_p = jnp.pad(router_w, ((0, 0), (0, LANES - N_EXPERTS))).astype(BF16)
    router_b_p = jnp.pad(router_b.astype(F32), (0, LANES - N_EXPERTS)).reshape(1, LANES)
    mem_prompt_bf16 = mem_prompt.reshape(bp * n_mem, d).astype(BF16)
    logf_pages = cache_fox[..., 2 * HEAD_DIM:].transpose(0, 1, 3, 2)

    xp, xp_b = _layer_norm(x_prompt.reshape(bp * tp, d), ln_in_g, ln_in_b)
    xs, xs_b = _layer_norm(x_sample.reshape(bs * ts, d), ln_in_g, ln_in_b)

    outs_p, outs_s, mem_ks, mem_vs = [], [], [], []
    for l in range(depth):
        lam_init = 0.8 - 0.6 * math.exp(-0.3 * l)
        lam = _diff_lambda(diff_lambda[l], lam_init)
        wl = _prep_weights(l, w_in, mla_w_q_up, mla_w_uk, mla_w_uv, w_branch, w_gate, b_gate, w_out, mem_w_q,
                           mem_w_k, mem_w_v, mem_w_o, moe_w1, moe_w3, moe_w2)
        mk = _matmul(mem_prompt_bf16, wl["mem_w_k"]).reshape(bp, n_mem, MEM_WIDTH)
        mv = _matmul(mem_prompt_bf16, wl["mem_w_v"]).reshape(bp, n_mem, MEM_WIDTH)

        pr = _project_inputs(xp_b, bp, tp, pos_p, wl, mla_g_q[l], mla_g_kv[l], fox_b_f[l])
        branches = _prompt_mixers(pr, bp, tp, wl, lam, diff_g_norm[l], lam_init)
        xp, xp_b = _finish_layer(xp, xp_b, branches, mk, mv, bp, tp, wl, ln_g[l], ln_b[l], router_w_p, router_b_p, alpha)
        outs_p.append(pr)

        ps = _project_inputs(xs_b, bs, ts, pos_s, wl, mla_g_q[l], mla_g_kv[l], fox_b_f[l])
        branches = _sample_mixers(ps, bs, wl, lam, diff_g_norm[l], lam_init, l,
                                  (cache_mla, cache_moba, cache_diff, cache_fox), logf_pages, pt_flat, n_pages)
        xs, xs_b = _finish_layer(xs, xs_b, branches, cache_mem_k[l].reshape(bs, n_mem, MEM_WIDTH),
                                 cache_mem_v[l].reshape(bs, n_mem, MEM_WIDTH), bs, ts, wl, ln_g[l], ln_b[l],
                                 router_w_p, router_b_p, alpha)
        outs_s.append(ps)
        mem_ks.append(mk.reshape(bp, n_mem, MEM_HEADS, MEM_HEAD_DIM))
        mem_vs.append(mv.reshape(bp, n_mem, MEM_HEADS, MEM_HEAD_DIM))

    rows = lambda outs, name, b, t: jnp.stack([o[name].reshape(b, t, -1) for o in outs])
    return (xp.reshape(bp, tp, d), xs.reshape(bs, ts, d),
            rows(outs_p, "rows_a", bp, tp), rows(outs_p, "rows_b", bp, tp),
            rows(outs_p, "rows_c", bp, tp), rows(outs_p, "rows_d", bp, tp),
            jnp.stack(mem_ks), jnp.stack(mem_vs),
            rows(outs_s, "rows_a", bs, ts), rows(outs_s, "rows_b", bs, ts),
            rows(outs_s, "rows_c", bs, ts), rows(outs_s, "rows_d", bs, ts))
```

```python
import functools
import math

import numpy as np
import jax
import jax.numpy as jnp
from jax import lax
from jax.experimental import pallas as pl
from jax.experimental.pallas import tpu as pltpu

F32 = jnp.float32
BF16 = jnp.bfloat16

D_MODEL = 2048
N_HEADS = 8
HEAD_DIM = 64
BRANCH_WIDTH = N_HEADS * HEAD_DIM
N_BRANCHES = 4
MLA_Q_LORA = 384
MLA_KV_LORA = 128
MLA_NOPE_DIM = HEAD_DIM
MLA_ROPE_DIM = 32
ROPE_THETA = 10000.0
MOBA_BLOCK = 256
MOBA_TOPK = 3
DIFF_QK_DIM = HEAD_DIM // 2
DIFF_V_DIM = HEAD_DIM
MEM_HEADS = 4
MEM_HEAD_DIM = 128
MEM_WIDTH = MEM_HEADS * MEM_HEAD_DIM
N_EXPERTS = 16
EXPERTS_PER_GROUP = 4
D_EXPERT = 512
PAGE_SIZE = 128
LN_EPS = 1e-5
RMS_EPS = 1e-6
NEG_INF = -1e30
BELOW_NEG_INF = -3e38

MLA_COLS = MLA_Q_LORA + MLA_KV_LORA + MLA_ROPE_DIM
MOBA_COLS = BRANCH_WIDTH + 2 * HEAD_DIM
DIFF_COLS = N_HEADS * 2 * DIFF_QK_DIM + 2 * DIFF_QK_DIM + DIFF_V_DIM
FOX_COLS = BRANCH_WIDTH + 2 * HEAD_DIM + N_HEADS
IN_COLS = MLA_COLS + MOBA_COLS + DIFF_COLS + FOX_COLS

LANES = 128
VMEM_LIMIT_BYTES = 56 * 1024 * 1024

ATTN_BLOCK = 512
LOG2E = 1.4426950408889634
DECODE_PAGES = 16

_NT = (((1,), (1,)), ((), ()))


def _cparams(*sem):
    return pltpu.CompilerParams(dimension_semantics=sem, vmem_limit_bytes=VMEM_LIMIT_BYTES)


def _tile(n, candidates):
    for c in candidates:
        if n % c == 0:
            return c
    return n


def _layer_norm_rows(y, g, b):
    mu = jnp.mean(y, -1, keepdims=True)
    yc = y - mu
    var = jnp.mean(yc * yc, -1, keepdims=True)
    return yc * lax.rsqrt(var + LN_EPS) * g + b


def _ln_body(x_ref, g_ref, b_ref, o_ref, ob_ref):
    y = _layer_norm_rows(x_ref[...], g_ref[...], b_ref[...])
    o_ref[...] = y
    ob_ref[...] = y.astype(BF16)


def _layer_norm(x, g, b):
    m, d = x.shape
    tm = _tile(m, (512, 256, 128))
    return pl.pallas_call(
        _ln_body,
        grid=(m // tm,),
        in_specs=[pl.BlockSpec((tm, d), lambda i: (i, 0)),
                  pl.BlockSpec((1, d), lambda i: (0, 0)),
                  pl.BlockSpec((1, d), lambda i: (0, 0))],
        out_specs=[pl.BlockSpec((tm, d), lambda i: (i, 0)),
                   pl.BlockSpec((tm, d), lambda i: (i, 0))],
        out_shape=[jax.ShapeDtypeStruct((m, d), F32), jax.ShapeDtypeStruct((m, d), BF16)],
        compiler_params=_cparams("parallel"),
        name="layer_norm",
    )(x, g.reshape(1, d), b.reshape(1, d))


def _mm_body(x_ref, w_ref, o_ref):
    o_ref[...] = jnp.dot(x_ref[...].astype(BF16), w_ref[...], preferred_element_type=F32).astype(o_ref.dtype)


def _mm_rms_body(x_ref, g_ref, w_ref, o_ref):
    x = x_ref[...]
    xn = x * lax.rsqrt(jnp.mean(x * x, -1, keepdims=True) + RMS_EPS) * g_ref[...]
    o_ref[...] = jnp.dot(xn.astype(BF16), w_ref[...], preferred_element_type=F32).astype(o_ref.dtype)


def _matmul(x, w, out_dtype=F32, rms_gain=None):
    m, k = x.shape
    n = w.shape[1]
    tm = _tile(m, (512, 256, 128))
    tn = _tile(n, (1024, 768, 640, 512, 384, 256, 128))
    x_spec = pl.BlockSpec((tm, k), lambda i, j: (i, 0))
    w_spec = pl.BlockSpec((k, tn), lambda i, j: (0, j))
    o_spec = pl.BlockSpec((tm, tn), lambda i, j: (i, j))
    if rms_gain is None:
        body, specs, args = _mm_body, [x_spec, w_spec], (x, w)
    else:
        body = _mm_rms_body
        specs = [x_spec, pl.BlockSpec((1, k), lambda i, j: (0, 0)), w_spec]
        args = (x, rms_gain.reshape(1, k), w)
    return pl.pallas_call(
        body, grid=(m // tm, n // tn), in_specs=specs, out_specs=o_spec,
        out_shape=jax.ShapeDtypeStruct((m, n), out_dtype),
        compiler_params=_cparams("parallel", "arbitrary"),
        name="matmul",
    )(*args)


def _bmm_body(x_ref, w_ref, o_ref):
    o_ref[...] = jnp.dot(x_ref[...], w_ref[...], preferred_element_type=F32).astype(o_ref.dtype)


def _head_matmul(x, w, out_dtype):
    h, m, k = x.shape
    n = w.shape[2]
    tm = _tile(m, (1024, 512, 256, 128))
    return pl.pallas_call(
        _bmm_body, grid=(h, m // tm),
        in_specs=[pl.BlockSpec((None, tm, k), lambda a, i: (a, i, 0)),
                  pl.BlockSpec((None, k, n), lambda a, i: (a, 0, 0))],
        out_specs=pl.BlockSpec((None, tm, n), lambda a, i: (a, i, 0)),
        out_shape=jax.ShapeDtypeStruct((h, m, n), out_dtype),
        compiler_params=_cparams("parallel", "arbitrary"),
        name="head_matmul",
    )(x, w)


def _mm_ln_body(x_ref, w_ref, r_ref, g_ref, b_ref, o_ref, ob_ref, *, alpha):
    y = alpha * r_ref[...] + jnp.dot(x_ref[...], w_ref[...], preferred_element_type=F32)
    y = _layer_norm_rows(y, g_ref[...], b_ref[...])
    o_ref[...] = y
    ob_ref[...] = y.astype(BF16)


def _matmul_residual_ln(x, w, resid, g, b, alpha):
    m, k = x.shape
    d = w.shape[1]
    tm = _tile(m, (256, 128))
    row = lambda i: (i, 0)
    fixed = lambda i: (0, 0)
    return pl.pallas_call(
        functools.partial(_mm_ln_body, alpha=alpha),
        grid=(m // tm,),
        in_specs=[pl.BlockSpec((tm, k), row), pl.BlockSpec((k, d), fixed), pl.BlockSpec((tm, d), row),
                  pl.BlockSpec((1, d), fixed), pl.BlockSpec((1, d), fixed)],
        out_specs=[pl.BlockSpec((tm, d), row), pl.BlockSpec((tm, d), row)],
        out_shape=[jax.ShapeDtypeStruct((m, d), F32), jax.ShapeDtypeStruct((m, d), BF16)],
        compiler_params=_cparams("parallel"),
        name="matmul_residual_ln",
    )(x, w, resid, g.reshape(1, d), b.reshape(1, d))


def _merge_body(x_ref, wg_ref, bg_ref, o_ref, wb_ref, out_ref, acc_ref):
    n = pl.program_id(2)

    @pl.when(n == 0)
    def _():
        acc_ref[...] = jnp.zeros_like(acc_ref)

    gate = jax.nn.sigmoid(jnp.dot(x_ref[...], wg_ref[...], preferred_element_type=F32) + bg_ref[...])
    proj = jnp.dot(o_ref[...], wb_ref[...], preferred_element_type=F32)
    acc_ref[...] += gate * proj

    @pl.when(n == N_BRANCHES - 1)
    def _():
        out_ref[...] = acc_ref[...].astype(out_ref.dtype)


def _gated_merge(x, w_gate, b_gate, branches, w_branch):
    m, d = x.shape
    c = branches.shape[2]
    tm = _tile(m, (1024, 512, 256, 128))
    tn = 512
    return pl.pallas_call(
        _merge_body,
        grid=(m // tm, d // tn, N_BRANCHES),
        in_specs=[pl.BlockSpec((tm, d), lambda i, j, n: (i, 0)),
                  pl.BlockSpec((None, d, tn), lambda i, j, n: (n, 0, j)),
                  pl.BlockSpec((None, 1, tn), lambda i, j, n: (n, 0, j)),
                  pl.BlockSpec((None, tm, c), lambda i, j, n: (n, i, 0)),
                  pl.BlockSpec((None, c, tn), lambda i, j, n: (n, 0, j))],
        out_specs=pl.BlockSpec((tm, tn), lambda i, j, n: (i, j)),
        out_shape=jax.ShapeDtypeStruct((m, d), BF16),
        scratch_shapes=[pltpu.VMEM((tm, tn), F32)],
        compiler_params=_cparams("parallel", "arbitrary", "arbitrary"),
        name="gated_merge",
    )(x, w_gate, b_gate, branches, w_branch)


def _mem_attn_body(q_ref, k_ref, v_ref, o_ref):
    scale = MEM_HEAD_DIM ** -0.5
    outs = []
    for h in range(MEM_HEADS):
        sl = slice(h * MEM_HEAD_DIM, (h + 1) * MEM_HEAD_DIM)
        k = k_ref[:, sl].astype(BF16)
        v = v_ref[:, sl].astype(BF16)
        s = lax.dot_general(q_ref[:, sl], k, _NT, preferred_element_type=F32) * scale
        m = jnp.max(s, -1, keepdims=True)
        e = jnp.exp(s - m)
        p = e / jnp.sum(e, -1, keepdims=True)
        outs.append(jnp.dot(p.astype(BF16), v, preferred_element_type=F32))
    o_ref[...] = jnp.concatenate(outs, axis=-1).astype(o_ref.dtype)


def _mem_attention(q, mem_k, mem_v):
    b, t, w = q.shape
    nm = mem_k.shape[1]
    tq = _tile(t, (512, 256, 128))
    return pl.pallas_call(
        _mem_attn_body, grid=(b, t // tq),
        in_specs=[pl.BlockSpec((None, tq, w), lambda a, i: (a, i, 0)),
                  pl.BlockSpec((None, nm, w), lambda a, i: (a, 0, 0)),
                  pl.BlockSpec((None, nm, w), lambda a, i: (a, 0, 0))],
        out_specs=pl.BlockSpec((None, tq, w), lambda a, i: (a, i, 0)),
        out_shape=jax.ShapeDtypeStruct((b, t, w), BF16),
        compiler_params=_cparams("parallel", "arbitrary"),
        name="mem_attention",
    )(q, mem_k, mem_v)


def _router_body(x_ref, w_ref, b_ref, gate_ref):
    logits = jnp.dot(x_ref[...], w_ref[...], preferred_element_type=F32)
    scores = jax.nn.sigmoid(logits)
    sel = scores + b_ref[...]
    lane = lax.broadcasted_iota(jnp.int32, sel.shape, 1)
    in_grp = lane % EXPERTS_PER_GROUP
    grp = lane // EXPERTS_PER_GROUP
    real = lane < N_EXPERTS

    rank = jnp.zeros(sel.shape, jnp.int32)
    for d in range(1, EXPERTS_PER_GROUP):
        lower = pltpu.roll(sel, d, 1)
        upper = pltpu.roll(sel, LANES - d, 1)
        rank += jnp.where((in_grp >= d) & (lower >= sel), 1, 0)
        rank += jnp.where((in_grp + d < EXPERTS_PER_GROUP) & (upper > sel), 1, 0)
    top2 = rank < 2

    v = jnp.where(top2, sel, 0.0)
    gs = v
    for d in range(1, EXPERTS_PER_GROUP):
        gs += jnp.where(in_grp >= d, pltpu.roll(v, d, 1), 0.0)
        gs += jnp.where(in_grp + d < EXPERTS_PER_GROUP, pltpu.roll(v, LANES - d, 1), 0.0)

    n_groups = N_EXPERTS // EXPERTS_PER_GROUP
    beaten = jnp.zeros(sel.shape, jnp.int32)
    for d in range(1, n_groups):
        lower = pltpu.roll(gs, d * EXPERTS_PER_GROUP, 1)
        upper = pltpu.roll(gs, LANES - d * EXPERTS_PER_GROUP, 1)
        beaten += jnp.where((grp >= d) & (lower >= gs), 1, 0)
        beaten += jnp.where((grp + d < n_groups) & (upper > gs), 1, 0)
    chosen = top2 & (beaten == 0) & real

    w_sel = jnp.where(chosen, scores, 0.0)
    gate_ref[...] = w_sel / jnp.sum(w_sel, -1, keepdims=True)


def _router(x, router_w, router_b):
    m, d = x.shape
    tm = _tile(m, (512, 256, 128))
    return pl.pallas_call(
        _router_body, grid=(m // tm,),
        in_specs=[pl.BlockSpec((tm, d), lambda i: (i, 0)),
                  pl.BlockSpec((d, LANES), lambda i: (0, 0)),
                  pl.BlockSpec((1, LANES), lambda i: (0, 0))],
        out_specs=pl.BlockSpec((tm, LANES), lambda i: (i, 0)),
        out_shape=jax.ShapeDtypeStruct((m, LANES), F32),
        compiler_params=_cparams("parallel"),
        name="moe_router",
    )(x, router_w, router_b)


def _moe_body(x_ref, gate_ref, w1_ref, w3_ref, w2_ref, r_ref, g_ref, b_ref, o_ref, ob_ref, acc_ref, *, alpha):
    e = pl.program_id(1)

    @pl.when(e == 0)
    def _():
        acc_ref[...] = jnp.zeros_like(acc_ref)

    x = x_ref[...]
    h1 = jnp.dot(x, w1_ref[...], preferred_element_type=F32)
    h3 = jnp.dot(x, w3_ref[...], preferred_element_type=F32)
    gate = gate_ref[...]
    lane = lax.broadcasted_iota(jnp.int32, gate.shape, 1)
    ge = jnp.sum(jnp.where(lane == e, gate, 0.0), -1, keepdims=True)
    hdn = jax.nn.silu(h1) * h3 * ge
    acc_ref[...] += jnp.dot(hdn.astype(BF16), w2_ref[...], preferred_element_type=F32)

    @pl.when(e == N_EXPERTS - 1)
    def _():
        y = _layer_norm_rows(alpha * r_ref[...] + acc_ref[...], g_ref[...], b_ref[...])
        o_ref[...] = y
        ob_ref[...] = y.astype(BF16)


def _moe_residual_ln(x, gate, w1, w3, w2, resid, g, b, alpha):
    m, d = x.shape
    f = w1.shape[2]
    tm = _tile(m, (512, 256, 128))
    row = lambda i, e: (i, 0)
    fixed = lambda i, e: (0, 0)
    return pl.pallas_call(
        functools.partial(_moe_body, alpha=alpha),
        grid=(m // tm, N_EXPERTS),
        in_specs=[pl.BlockSpec((tm, d), row), pl.BlockSpec((tm, LANES), row),
                  pl.BlockSpec((None, d, f), lambda i, e: (e, 0, 0)),
                  pl.BlockSpec((None, d, f), lambda i, e: (e, 0, 0)),
                  pl.BlockSpec((None, f, d), lambda i, e: (e, 0, 0)),
                  pl.BlockSpec((tm, d), row), pl.BlockSpec((1, d), fixed), pl.BlockSpec((1, d), fixed)],
        out_specs=[pl.BlockSpec((tm, d), row), pl.BlockSpec((tm, d), row)],
        out_shape=[jax.ShapeDtypeStruct((m, d), F32), jax.ShapeDtypeStruct((m, d), BF16)],
        scratch_shapes=[pltpu.VMEM((tm, d), F32)],
        compiler_params=_cparams("parallel", "arbitrary"),
        name="moe_experts",
    )(x, gate, w1, w3, w2, resid, g.reshape(1, d), b.reshape(1, d))


def _lane_prefix_sum(x):
    lane = lax.broadcasted_iota(jnp.int32, x.shape, 1)
    shift = 1
    while shift < LANES:
        x = x + jnp.where(lane >= shift, pltpu.roll(x, shift, 1), 0.0)
        shift *= 2
    return x


def _cumsum_body(*refs, pieces, n_prefetch):
    x_refs, o_ref, carry_ref = refs[n_prefetch:-2], refs[-2], refs[-1]

    @pl.when(pl.program_id(1) == 0)
    def _():
        carry_ref[...] = jnp.zeros_like(carry_ref)

    carry = carry_ref[...]
    for p in range(pieces):
        x = x_refs[p][...] if len(x_refs) > 1 else x_refs[0][:, p * LANES:(p + 1) * LANES]
        c = _lane_prefix_sum(x) + carry
        o_ref[:, p * LANES:(p + 1) * LANES] = c
        carry = c[:, LANES - 1:LANES]
    carry_ref[...] = carry


def _cumsum_rows(x):
    b, r, l = x.shape
    pieces = _tile(l // LANES, (16, 8, 4, 2, 1))
    w = pieces * LANES
    return pl.pallas_call(
        functools.partial(_cumsum_body, pieces=pieces, n_prefetch=0),
        grid=(b, l // w),
        in_specs=[pl.BlockSpec((None, r, w), lambda a, j: (a, 0, j))],
        out_specs=pl.BlockSpec((None, r, w), lambda a, j: (a, 0, j)),
        out_shape=jax.ShapeDtypeStruct((b, r, l), F32),
        scratch_shapes=[pltpu.VMEM((r, 1), F32)],
        compiler_params=_cparams("parallel", "arbitrary"),
        name="cumsum_rows",
    )(x)


def _block_mean_body(k_ref, o_ref):
    o_ref[...] = jnp.sum(k_ref[...], 0, keepdims=True) * (1.0 / MOBA_BLOCK)


def _moba_block_means(k):
    b, l, d = k.shape
    nb = l // MOBA_BLOCK
    out = pl.pallas_call(
        _block_mean_body, grid=(b, nb),
        in_specs=[pl.BlockSpec((None, MOBA_BLOCK, d), lambda a, n: (a, n, 0))],
        out_specs=pl.BlockSpec((None, None, 1, d), lambda a, n: (a, n, 0, 0)),
        out_shape=jax.ShapeDtypeStruct((b, nb, 1, d), F32),
        compiler_params=_cparams("parallel", "arbitrary"),
        name="moba_block_means",
    )(k)
    return out.reshape(b, nb, d)


def _paged_block_mean_body(pt_ref, *refs):
    page_refs, o_ref = refs[:-1], refs[-1]
    per_block = MOBA_BLOCK // PAGE_SIZE
    for n in range(len(page_refs) // per_block):
        tot = jnp.sum(page_refs[n * per_block][...], 0, keepdims=True)
        for t in range(1, per_block):
            tot = tot + jnp.sum(page_refs[n * per_block + t][...], 0, keepdims=True)
        o_ref[n] = tot * (1.0 / MOBA_BLOCK)


def _moba_block_means_paged(cache, layer, page_table_flat, n_seq, n_pages):
    w = cache.shape[3]
    pp = DECODE_PAGES
    bps = pp * PAGE_SIZE // MOBA_BLOCK
    nb = n_pages * PAGE_SIZE // MOBA_BLOCK
    specs = [pl.BlockSpec((None, None, PAGE_SIZE, w),
                          functools.partial(lambda a, j, pt, p: (layer, pt[a * n_pages + j * pp + p], 0, 0), p=p))
             for p in range(pp)]
    out = pl.pallas_call(
        _paged_block_mean_body,
        grid_spec=pltpu.PrefetchScalarGridSpec(
            num_scalar_prefetch=1, grid=(n_seq, n_pages // pp),
            in_specs=specs,
            out_specs=pl.BlockSpec((None, bps, 1, w), lambda a, j, pt: (a, j, 0, 0))),
        out_shape=jax.ShapeDtypeStruct((n_seq, nb, 1, w), F32),
        compiler_params=_cparams("parallel", "arbitrary"),
        name="moba_block_means_paged",
    )(page_table_flat, *([cache] * pp))
    return out.reshape(n_seq, nb, w)


def _top_blocks(gate, limit):
    lane = lax.broadcasted_iota(jnp.int32, gate.shape, gate.ndim - 1)
    cur = jnp.where(lane < limit, gate, NEG_INF)
    picked = jnp.zeros(gate.shape, jnp.bool_)
    for _ in range(MOBA_TOPK):
        cur = jnp.where(picked, BELOW_NEG_INF, cur)
        best = jnp.max(cur, -1, keepdims=True)
        first = jnp.min(jnp.where(cur == best, lane, LANES), -1, keepdims=True)
        picked = picked | (lane == first)
    return picked & (lane < limit)


def _flash_body(qi_ref, kj_ref, *refs, mode, n_heads, blk, scale, lam_init):
    q_ref, k_ref, v_ref = refs[:3]
    refs = refs[3:]
    if mode == "fox":
        ck_ref = refs[0]
        refs = refs[1:]
    if mode in ("moba", "diff"):
        slope_ref = refs[0]
        refs = refs[1:]
    if mode == "moba":
        kmean_ref = refs[0]
        refs = refs[1:]
    if mode == "diff":
        lam_ref, gn_ref = refs[:2]
        refs = refs[2:]
    o_ref, m_ref, acc_ref = refs[:3]
    sel_ref = refs[3] if mode == "moba" else None

    t = pl.program_id(1)
    i = qi_ref[t]
    j = kj_ref[t]
    sub = blk // MOBA_BLOCK
    dk = q_ref.shape[-1]
    dv = o_ref.shape[-1]
    rows = n_heads * blk

    row = lax.broadcasted_iota(jnp.int32, (1, blk, blk), 1)
    col = lax.broadcasted_iota(jnp.int32, (1, blk, blk), 2)

    @pl.when(j == 0)
    def _():
        m_ref[...] = jnp.full_like(m_ref, NEG_INF)
        acc_ref[...] = jnp.zeros_like(acc_ref)
        if mode == "moba":
            own = (i * blk + lax.broadcasted_iota(jnp.int32, (1, blk, 1), 1)) // MOBA_BLOCK
            gate = lax.dot_general(q_ref[...].reshape(rows, dk), kmean_ref[...], _NT,
                                   preferred_element_type=F32).reshape(n_heads, blk, LANES)
            sel_ref[...] = jnp.where(_top_blocks(gate, own), 1.0, 0.0)

    def tile(diag):
        s = lax.dot_general(q_ref[...].reshape(rows, dk), k_ref[...], _NT, preferred_element_type=F32)
        s = s.reshape(n_heads, blk, blk) * (scale * LOG2E)
        if mode in ("moba", "diff"):
            key_off = ((j - i) * blk + lax.broadcasted_iota(jnp.int32, (1, 1, blk), 2)).astype(F32)
            s = s + (slope_ref[...] * LOG2E) * key_off
        if mode == "fox":
            s = s - ck_ref[...] * LOG2E
        if mode == "moba":
            sel = sel_ref[...]
            lane = lax.broadcasted_iota(jnp.int32, (1, 1, LANES), 2)
            picked = None
            for u in range(sub):
                on = jnp.sum(jnp.where(lane == j * sub + u, sel, 0.0), -1, keepdims=True) > 0.0
                if sub > 1:
                    on = on & (col // MOBA_BLOCK == u)
                picked = on if picked is None else picked | on
            if diag:
                own_blk = row // MOBA_BLOCK
                key_blk = col // MOBA_BLOCK
                valid = ((key_blk == own_blk) & (row >= col)) | ((key_blk < own_blk) & picked)
            else:
                valid = picked
            s = jnp.where(valid, s, NEG_INF)
        elif diag:
            s = jnp.where(row >= col, s, NEG_INF)
        m_prev = m_ref[...]
        m_new = jnp.maximum(m_prev, jnp.max(s, -1, keepdims=True))
        p = jnp.exp2(s - m_new)
        pv = jnp.dot(p.astype(BF16).reshape(rows, blk), v_ref[...], preferred_element_type=F32)
        acc_ref[...] = jnp.exp2(m_prev - m_new) * acc_ref[...] + pv.reshape(acc_ref.shape)
        m_ref[...] = m_new

    @pl.when(j < i)
    def _():
        tile(False)

    @pl.when(j == i)
    def _():
        tile(True)
        acc = acc_ref[...]
        out = acc[..., :dv] / acc[..., dv:dv + 1]
        if mode == "diff":
            a = out[:N_HEADS] - lam_ref[0, 0] * out[N_HEADS:]
            a = a * lax.rsqrt(jnp.mean(a * a, -1, keepdims=True) + RMS_EPS) * gn_ref[...]
            out = a * (1.0 - lam_init)
        o_ref[...] = out.astype(o_ref.dtype)


def _flash_attention(mode, q, k, v, scale, *, c=None, kmean=None, lam=None, g_norm=None, lam_init=0.0):
    nh, b, t, dk = q.shape
    dv = v.shape[2]
    blk = ATTN_BLOCK
    nq = t // blk
    pairs = [(i, j) for i in range(nq) for j in range(i + 1)]
    qi = jnp.asarray(np.array([p[0] for p in pairs], np.int32))
    kj = jnp.asarray(np.array([p[1] for p in pairs], np.int32))
    h_out = N_HEADS if mode == "diff" else nh
    dva = -(-(dv + 1) // LANES) * LANES
    v = jnp.concatenate([v, jnp.ones((b, t, 1), BF16), jnp.zeros((b, t, dva - dv - 1), BF16)], -1)

    in_specs = [pl.BlockSpec((nh, None, blk, dk), lambda a, s, qi, kj: (0, a, qi[s], 0)),
                pl.BlockSpec((None, blk, dk), lambda a, s, qi, kj: (a, kj[s], 0)),
                pl.BlockSpec((None, blk, dva), lambda a, s, qi, kj: (a, kj[s], 0))]
    args = [q, k, v]
    scratch = [pltpu.VMEM((nh, blk, 1), F32), pltpu.VMEM((nh, blk, dva), F32)]
    if mode == "fox":
        in_specs.append(pl.BlockSpec((None, nh, 1, blk), lambda a, s, qi, kj: (a, 0, 0, kj[s])))
        args.append(c)
    if mode in ("moba", "diff"):
        in_specs.append(pl.BlockSpec((nh, 1, 1), lambda a, s, qi, kj: (0, 0, 0)))
        args.append(jnp.asarray(np.array([2.0 ** -(h % N_HEADS + 1) for h in range(nh)], np.float32).reshape(nh, 1, 1)))
    if mode == "moba":
        in_specs.append(pl.BlockSpec((None, LANES, dk), lambda a, s, qi, kj: (a, 0, 0)))
        args.append(kmean)
        scratch.append(pltpu.VMEM((nh, blk, LANES), F32))
    if mode == "diff":
        in_specs += [pl.BlockSpec(memory_space=pltpu.SMEM),
                     pl.BlockSpec((N_HEADS, 1, dv), lambda a, s, qi, kj: (0, 0, 0))]
        args += [lam.reshape(1, 1), g_norm]

    return pl.pallas_call(
        functools.partial(_flash_body, mode=mode, n_heads=nh, blk=blk, scale=scale, lam_init=lam_init),
        grid_spec=pltpu.PrefetchScalarGridSpec(
            num_scalar_prefetch=2, grid=(b, len(pairs)), in_specs=in_specs,
            out_specs=pl.BlockSpec((h_out, None, blk, dv), lambda a, s, qi, kj: (0, a, qi[s], 0)),
            scratch_shapes=scratch),
        out_shape=jax.ShapeDtypeStruct((h_out, b, t, dv), BF16),
        compiler_params=_cparams("parallel", "arbitrary"),
        name="flash_" + mode,
    )(qi, kj, *args)


def _page_prefix_sums(x):
    hi = x.astype(BF16)
    rest = x - hi.astype(F32)
    mid = rest.astype(BF16)
    low = (rest - mid.astype(F32)).astype(BF16)
    src = lax.broadcasted_iota(jnp.int32, (LANES, LANES), 0)
    dst = lax.broadcasted_iota(jnp.int32, (LANES, LANES), 1)
    tri = jnp.where(src <= dst, 1.0, 0.0).astype(BF16)
    dot = lambda a: jnp.dot(a, tri, preferred_element_type=F32)
    return dot(hi) + dot(mid) + dot(low)


def _decode_body(pt_ref, *refs, mode, scale, past_len, lam_init):
    pp = DECODE_PAGES
    keys_on_lanes = mode in ("mla", "fox")
    q_ref, new_ref = refs[:2]
    page_refs = refs[2:2 + pp]
    refs = refs[2 + pp:]
    if mode == "fox":
        logf_new_ref = refs[0]
        refs = refs[1:]
    if mode == "diff":
        lam_ref, gn_ref, slope_ref = refs[:3]
        refs = refs[3:]
    o_ref, m_ref, l_ref, acc_ref = refs[:4]

    j = pl.program_id(1)
    span = pp * PAGE_SIZE

    @pl.when(j == 0)
    def _():
        m_ref[...] = jnp.full_like(m_ref, NEG_INF)
        l_ref[...] = jnp.zeros_like(l_ref)
        acc_ref[...] = jnp.zeros_like(acc_ref)
        if mode == "fox":
            refs[4][...] = jnp.zeros_like(refs[4])

    q = q_ref[...]
    keys = [page_refs[p][...].astype(BF16) for p in range(pp)]
    if keys_on_lanes:
        s = jnp.concatenate([jnp.dot(q, kt, preferred_element_type=F32) for kt in keys], axis=1) * scale
    else:
        s = jnp.concatenate([lax.dot_general(q, kp, _NT, preferred_element_type=F32) for kp in keys], axis=1) * scale
    if mode == "diff":
        dist = (past_len - j * span - lax.broadcasted_iota(jnp.int32, (1, span), 1)).astype(F32)
        s = s - slope_ref[...] * dist
    if mode == "fox":
        carry_ref = refs[4]
        logf = jnp.concatenate([page_refs[p][2 * HEAD_DIM:2 * HEAD_DIM + N_HEADS, :] for p in range(pp)], axis=0)
        within = _page_prefix_sums(logf)
        carry = carry_ref[...]
        pieces = []
        for p in range(pp):
            w = within[p * N_HEADS:(p + 1) * N_HEADS]
            pieces.append(w + carry)
            carry = carry + w[:, LANES - 1:LANES]
        carry_ref[...] = carry
        s = s - jnp.concatenate(pieces, axis=1)
    m_prev = m_ref[...]
    m_new = jnp.maximum(m_prev, jnp.max(s, -1, keepdims=True))
    alpha = jnp.exp(m_prev - m_new)
    p = jnp.exp(s - m_new)
    l_ref[...] = alpha * l_ref[...] + jnp.sum(p, -1, keepdims=True)
    pb = p.astype(BF16)
    acc = alpha * acc_ref[...]
    for u in range(pp):
        pu = pb[:, u * PAGE_SIZE:(u + 1) * PAGE_SIZE]
        if keys_on_lanes:
            acc = acc + lax.dot_general(pu, keys[u][:LANES, :], _NT, preferred_element_type=F32)
        else:
            acc = acc + jnp.dot(pu, keys[u][:, :LANES], preferred_element_type=F32)
    acc_ref[...] = acc
    m_ref[...] = m_new

    @pl.when(j == pl.num_programs(1) - 1)
    def _():
        new = new_ref[...].astype(BF16).astype(F32)
        s_new = jnp.sum(q.astype(F32) * new, -1, keepdims=True) * scale
        if mode == "fox":
            s_new = s_new - (refs[4][...] + logf_new_ref[...])
        m_old = m_ref[...]
        m_fin = jnp.maximum(m_old, s_new)
        a_old = jnp.exp(m_old - m_fin)
        p_new = jnp.exp(s_new - m_fin)
        l_fin = a_old * l_ref[...] + p_new
        acc_fin = a_old * acc_ref[...] + p_new.astype(BF16).astype(F32) * new[:, :LANES]
        out = acc_fin / l_fin
        if mode == "diff":
            a = out[:N_HEADS, DIFF_V_DIM:] - lam_ref[0, 0] * out[N_HEADS:, DIFF_V_DIM:]
            a = a * lax.rsqrt(jnp.mean(a * a, -1, keepdims=True) + RMS_EPS) * gn_ref[...]
            o_ref[...] = (a * (1.0 - lam_init)).astype(o_ref.dtype)
        else:
            o_ref[...] = out.astype(o_ref.dtype)


def _decode_attention(mode, q, new_rows, cache, layer, page_table_flat, scale, *, logf_new=None,
                      lam=None, g_norm=None, lam_init=0.0):
    b, n_rows, width = q.shape
    n_pages = page_table_flat.shape[0] // b
    pp = DECODE_PAGES
    past_len = n_pages * PAGE_SIZE
    page_block = (None, None, width, PAGE_SIZE) if mode in ("mla", "fox") else (None, None, PAGE_SIZE, width)
    in_specs = [pl.BlockSpec((None, n_rows, width), lambda a, j, pt: (a, 0, 0)),
                pl.BlockSpec((None, 1, width), lambda a, j, pt: (a, 0, 0))]
    in_specs += [pl.BlockSpec(page_block,
                              functools.partial(lambda a, j, pt, p: (layer, pt[a * n_pages + j * pp + p], 0, 0), p=p))
                 for p in range(pp)]
    args = [q, new_rows] + [cache] * pp
    out_rows, out_w = n_rows, LANES
    scratch = [pltpu.VMEM((n_rows, 1), F32), pltpu.VMEM((n_rows, 1), F32), pltpu.VMEM((n_rows, LANES), F32)]
    if mode == "fox":
        in_specs.append(pl.BlockSpec((None, n_rows, 1), lambda a, j, pt: (a, 0, 0)))
        args.append(logf_new)
        scratch.append(pltpu.VMEM((n_rows, 1), F32))
    if mode == "diff":
        in_specs += [pl.BlockSpec(memory_space=pltpu.SMEM),
                     pl.BlockSpec((N_HEADS, DIFF_V_DIM), lambda a, j, pt: (0, 0)),
                     pl.BlockSpec((n_rows, 1), lambda a, j, pt: (0, 0))]
        slopes = jnp.asarray(np.array([2.0 ** -(r % N_HEADS + 1) for r in range(n_rows)], np.float32).reshape(n_rows, 1))
        args += [lam.reshape(1, 1), g_norm, slopes]
        out_rows, out_w = N_HEADS, DIFF_V_DIM
    return pl.pallas_call(
        functools.partial(_decode_body, mode=mode, scale=scale, past_len=past_len, lam_init=lam_init),
        grid_spec=pltpu.PrefetchScalarGridSpec(
            num_scalar_prefetch=1, grid=(b, n_pages // pp), in_specs=in_specs,
            out_specs=pl.BlockSpec((None, out_rows, out_w), lambda a, j, pt: (a, 0, 0)),
            scratch_shapes=scratch),
        out_shape=jax.ShapeDtypeStruct((b, out_rows, out_w), BF16),
        compiler_params=_cparams("parallel", "arbitrary"),
        name="decode_" + mode,
    )(page_table_flat, *args)


def _moba_pick_body(q_ref, km_ref, o_ref, *, n_blocks):
    gate = lax.dot_general(q_ref[...], km_ref[...].astype(BF16), _NT, preferred_element_type=F32)
    lane = lax.broadcasted_iota(jnp.int32, gate.shape, 1)
    cur = jnp.where(lane < n_blocks, gate, BELOW_NEG_INF)
    out = jnp.zeros(gate.shape, jnp.int32)
    for n in range(MOBA_TOPK):
        best = jnp.max(cur, -1, keepdims=True)
        first = jnp.min(jnp.where(cur == best, lane, LANES), -1, keepdims=True)
        out = jnp.where(lane == n, first, out)
        cur = jnp.where(lane == first, BELOW_NEG_INF, cur)
    o_ref[...] = out


def _moba_pick(q, kmean_padded, n_blocks):
    b, r, w = q.shape
    return pl.pallas_call(
        functools.partial(_moba_pick_body, n_blocks=n_blocks), grid=(b,),
        in_specs=[pl.BlockSpec((None, r, w), lambda a: (a, 0, 0)),
                  pl.BlockSpec((None, LANES, w), lambda a: (a, 0, 0))],
        out_specs=pl.BlockSpec((None, r, LANES), lambda a: (a, 0, 0)),
        out_shape=jax.ShapeDtypeStruct((b, r, LANES), jnp.int32),
        compiler_params=_cparams("parallel"),
        name="moba_pick",
    )(q, kmean_padded)


def _moba_decode_body(pg_ref, blk_ref, q_ref, new_ref, *refs, scale, past_len):
    page_refs, o_ref = refs[:-1], refs[-1]
    a = pl.program_id(0)
    per_block = MOBA_BLOCK // PAGE_SIZE
    per_head = MOBA_TOPK * per_block
    n_sel = N_HEADS * per_head
    q = q_ref[...]
    row = lax.broadcasted_iota(jnp.int32, (N_HEADS, PAGE_SIZE), 0)
    lane = lax.broadcasted_iota(jnp.int32, (N_HEADS, PAGE_SIZE), 1)
    slope = jnp.exp2(-(row + 1).astype(F32))
    keys, scores = [], []
    for u in range(n_sel):
        h, rest = divmod(u, per_head)
        n, half = divmod(rest, per_block)
        kp = page_refs[u][...].astype(BF16)
        keys.append(kp)
        s = lax.dot_general(q, kp, _NT, preferred_element_type=F32) * scale
        first_pos = blk_ref[a * N_HEADS * MOBA_TOPK + h * MOBA_TOPK + n] * MOBA_BLOCK + half * PAGE_SIZE
        dist = (past_len - first_pos - lane).astype(F32)
        scores.append(jnp.where(row == h, s - slope * dist, NEG_INF))
    s = jnp.concatenate(scores, axis=1)
    new = new_ref[...].astype(BF16).astype(F32)
    s_new = jnp.sum(q.astype(F32) * new, -1, keepdims=True) * scale
    m = jnp.maximum(jnp.max(s, -1, keepdims=True), s_new)
    p = jnp.exp(s - m)
    p_new = jnp.exp(s_new - m)
    denom = jnp.sum(p, -1, keepdims=True) + p_new
    pb = p.astype(BF16)
    acc = p_new.astype(BF16).astype(F32) * new
    for u in range(n_sel):
        acc = acc + jnp.dot(pb[:, u * PAGE_SIZE:(u + 1) * PAGE_SIZE], keys[u], preferred_element_type=F32)
    o_ref[...] = (acc / denom).astype(o_ref.dtype)


def _moba_decode_attention(q, new_rows, cache, layer, sel_pages_flat, sel_blocks_flat, scale, past_len):
    b, r, w = q.shape
    n_sel = N_HEADS * MOBA_TOPK * (MOBA_BLOCK // PAGE_SIZE)
    specs = [pl.BlockSpec((None, r, w), lambda a, pg, bk: (a, 0, 0)),
             pl.BlockSpec((None, 1, w), lambda a, pg, bk: (a, 0, 0))]
    specs += [pl.BlockSpec((None, None, PAGE_SIZE, w),
                           functools.partial(lambda a, pg, bk, u: (layer, pg[a * n_sel + u], 0, 0), u=u))
              for u in range(n_sel)]
    return pl.pallas_call(
        functools.partial(_moba_decode_body, scale=scale, past_len=past_len),
        grid_spec=pltpu.PrefetchScalarGridSpec(
            num_scalar_prefetch=2, grid=(b,), in_specs=specs,
            out_specs=pl.BlockSpec((None, r, w), lambda a, pg, bk: (a, 0, 0))),
        out_shape=jax.ShapeDtypeStruct((b, r, w), BF16),
        compiler_params=_cparams("arbitrary"),
        name="decode_moba",
    )(sel_pages_flat, sel_blocks_flat, q, new_rows, *([cache] * n_sel))


def _rope_tables(pos):
    half = MLA_ROPE_DIM // 2
    freqs = ROPE_THETA ** (-jnp.arange(half, dtype=F32) / half)
    ang = pos.astype(F32)[:, None] * freqs
    return jnp.cos(ang), jnp.sin(ang)


def _rope(x, cos, sin):
    half = x.shape[-1] // 2
    shape = (1, cos.shape[0]) + (1,) * (x.ndim - 3) + (half,)
    cos, sin = cos.reshape(shape), sin.reshape(shape)
    x1, x2 = x[..., :half], x[..., half:]
    return jnp.concatenate([x1 * cos - x2 * sin, x1 * sin + x2 * cos], -1)


def _heads_first(x, b, t, n, d):
    return x.reshape(b, t, n, d).transpose(2, 0, 1, 3)


def _prep_weights(l, w_in, mla_w_q_up, mla_w_uk, mla_w_uv, w_branch, w_gate, b_gate, w_out, mem_w_q, mem_w_k,
                  mem_w_v, mem_w_o, moe_w1, moe_w3, moe_w2):
    pad = (-IN_COLS) % LANES
    return dict(
        w_in=jnp.pad(w_in[l], ((0, 0), (0, pad))).astype(BF16),
        w_q_up=mla_w_q_up[l].reshape(MLA_Q_LORA, -1).astype(BF16),
        w_uk=mla_w_uk[l].transpose(1, 2, 0).astype(BF16),
        w_uv=mla_w_uv[l].transpose(1, 0, 2).astype(BF16),
        w_branch=w_branch[l].astype(BF16),
        w_gate=w_gate[l].reshape(D_MODEL, N_BRANCHES, D_MODEL).transpose(1, 0, 2).astype(BF16),
        b_gate=b_gate[l].reshape(N_BRANCHES, 1, D_MODEL),
        w_out=w_out[l].astype(BF16),
        mem_w_q=mem_w_q[l].astype(BF16), mem_w_k=mem_w_k[l].astype(BF16), mem_w_v=mem_w_v[l].astype(BF16),
        mem_w_o=mem_w_o[l].astype(BF16),
        moe_w1=moe_w1[l].astype(BF16), moe_w3=moe_w3[l].astype(BF16), moe_w2=moe_w2[l].astype(BF16),
    )


def _project_inputs(x_bf16, b, t, pos, wl, mla_g_q, mla_g_kv, fox_b_f):
    n = b * t
    h = _matmul(x_bf16, wl["w_in"])
    o = 0
    c_q, c_kv, k_r = h[:, :MLA_Q_LORA], h[:, MLA_Q_LORA:MLA_Q_LORA + MLA_KV_LORA], h[:, MLA_Q_LORA + MLA_KV_LORA:MLA_COLS]
    o += MLA_COLS
    q_b, rows_b = h[:, o:o + BRANCH_WIDTH], h[:, o + BRANCH_WIDTH:o + MOBA_COLS]
    o += MOBA_COLS
    q_c, rows_c = h[:, o:o + BRANCH_WIDTH], h[:, o + BRANCH_WIDTH:o + DIFF_COLS]
    o += DIFF_COLS
    q_d, kv_d, f_d = h[:, o:o + BRANCH_WIDTH], h[:, o + BRANCH_WIDTH:o + BRANCH_WIDTH + 2 * HEAD_DIM], h[:, o + BRANCH_WIDTH + 2 * HEAD_DIM:o + FOX_COLS]

    cos, sin = _rope_tables(pos)
    q = _matmul(c_q, wl["w_q_up"], rms_gain=mla_g_q).reshape(n, N_HEADS, MLA_NOPE_DIM + MLA_ROPE_DIM)
    q_nope = q[..., :MLA_NOPE_DIM].transpose(1, 0, 2).astype(BF16)
    q_lat = _head_matmul(q_nope, wl["w_uk"], BF16)
    q_rope = _rope(q[..., MLA_NOPE_DIM:].reshape(b, t, N_HEADS, MLA_ROPE_DIM), cos, sin)
    q_rope = q_rope.reshape(n, N_HEADS, MLA_ROPE_DIM).transpose(1, 0, 2).astype(BF16)
    q_a = jnp.concatenate([q_lat, q_rope], -1)
    c_kv_n = c_kv * lax.rsqrt(jnp.mean(c_kv * c_kv, -1, keepdims=True) + RMS_EPS) * mla_g_kv
    rows_a = jnp.concatenate([c_kv_n, _rope(k_r.reshape(b, t, MLA_ROPE_DIM), cos, sin).reshape(n, MLA_ROPE_DIM)], -1)
    log_f = jax.nn.log_sigmoid(f_d + fox_b_f)
    rows_d = jnp.concatenate([kv_d, log_f], -1)
    return dict(q_a=q_a, rows_a=rows_a, q_b=q_b, rows_b=rows_b, q_c=q_c, rows_c=rows_c, q_d=q_d, rows_d=rows_d,
                log_f=log_f)


def _diff_lambda(lam_p, lam_init):
    lam_p = lam_p.astype(F32)
    return jnp.exp(jnp.sum(lam_p[0] * lam_p[1])) - jnp.exp(jnp.sum(lam_p[2] * lam_p[3])) + lam_init


def _finish_layer(x, x_bf16, branches, mem_k, mem_v, b, t, wl, ln_g, ln_b, router_w, router_b, alpha):
    merged = _gated_merge(x_bf16, wl["w_gate"], wl["b_gate"], branches, wl["w_branch"])
    x, x_bf16 = _matmul_residual_ln(merged, wl["w_out"], x, ln_g[0], ln_b[0], alpha)
    q_m = _matmul(x_bf16, wl["mem_w_q"], out_dtype=BF16).reshape(b, t, MEM_WIDTH)
    o_m = _mem_attention(q_m, mem_k, mem_v).reshape(b * t, MEM_WIDTH)
    x, x_bf16 = _matmul_residual_ln(o_m, wl["mem_w_o"], x, ln_g[1], ln_b[1], alpha)
    gate = _router(x_bf16, router_w, router_b)
    return _moe_residual_ln(x_bf16, gate, wl["moe_w1"], wl["moe_w3"], wl["moe_w2"], x, ln_g[2], ln_b[2], alpha)


def _prompt_mixers(pr, b, t, wl, lam, g_norm, lam_init):
    n = b * t
    k_a = pr["rows_a"].reshape(b, t, -1).astype(BF16)
    o_lat = _flash_attention("mla", pr["q_a"].reshape(N_HEADS, b, t, -1), k_a, k_a[..., :MLA_KV_LORA],
                             (MLA_NOPE_DIM + MLA_ROPE_DIM) ** -0.5)
    o_a = _head_matmul(o_lat.reshape(N_HEADS, n, MLA_KV_LORA), wl["w_uv"], BF16)
    rows_b = pr["rows_b"].reshape(b, t, -1)
    kmean = _moba_block_means(rows_b[..., :HEAD_DIM])
    kmean = jnp.pad(kmean, ((0, 0), (0, LANES - kmean.shape[1]), (0, 0))).astype(BF16)
    kv_b = rows_b.astype(BF16)
    o_b = _flash_attention("moba", _heads_first(pr["q_b"], b, t, N_HEADS, HEAD_DIM).astype(BF16),
                           kv_b[..., :HEAD_DIM], kv_b[..., HEAD_DIM:], HEAD_DIM ** -0.5, kmean=kmean)
    q_c = pr["q_c"].reshape(b, t, N_HEADS, 2, DIFF_QK_DIM).transpose(3, 2, 0, 1, 4)
    zero = jnp.zeros_like(q_c[0])
    q_c = jnp.concatenate([jnp.concatenate([q_c[0], zero], -1), jnp.concatenate([zero, q_c[1]], -1)], 0).astype(BF16)
    kv_c = pr["rows_c"].reshape(b, t, -1).astype(BF16)
    o_c = _flash_attention("diff", q_c, kv_c[..., :2 * DIFF_QK_DIM], kv_c[..., 2 * DIFF_QK_DIM:], DIFF_QK_DIM ** -0.5,
                           lam=lam, g_norm=g_norm.reshape(N_HEADS, 1, DIFF_V_DIM), lam_init=lam_init)
    c = _cumsum_rows(pr["log_f"].reshape(b, t, N_HEADS).transpose(0, 2, 1)).reshape(b, N_HEADS, 1, t)
    kv_d = pr["rows_d"].reshape(b, t, -1).astype(BF16)
    o_d = _flash_attention("fox", _heads_first(pr["q_d"], b, t, N_HEADS, HEAD_DIM).astype(BF16),
                           kv_d[..., :HEAD_DIM], kv_d[..., HEAD_DIM:2 * HEAD_DIM], HEAD_DIM ** -0.5, c=c)
    to_tokens = lambda o: o.reshape(N_HEADS, n, HEAD_DIM).transpose(1, 0, 2).reshape(n, BRANCH_WIDTH)
    return jnp.stack([to_tokens(o_a), to_tokens(o_b), to_tokens(o_c), to_tokens(o_d)])


def _sample_mixers(pr, b, wl, lam, g_norm, lam_init, layer, caches, pt_flat, n_pages):
    cache_mla, cache_moba, cache_diff, cache_fox = caches
    past_len = n_pages * PAGE_SIZE
    per_tok = lambda q: q.transpose(1, 0, 2)
    o_lat = _decode_attention("mla", per_tok(pr["q_a"]), pr["rows_a"].reshape(b, 1, -1), cache_mla, layer, pt_flat,
                              (MLA_NOPE_DIM + MLA_ROPE_DIM) ** -0.5)
    o_a = _head_matmul(o_lat.transpose(1, 0, 2), wl["w_uv"], BF16).transpose(1, 0, 2)
    q_b = pr["q_b"].reshape(b, N_HEADS, HEAD_DIM)
    q_b = jnp.concatenate([q_b, jnp.zeros_like(q_b)], -1).astype(BF16)
    n_blocks = past_len // MOBA_BLOCK
    kmean = _moba_block_means_paged(cache_moba, layer, pt_flat, b, n_pages)
    kmean = jnp.pad(kmean, ((0, 0), (0, LANES - n_blocks), (0, 0)))
    choice = _moba_pick(q_b, kmean, n_blocks)[:, :, :MOBA_TOPK]
    per_block = MOBA_BLOCK // PAGE_SIZE
    page_pos = (choice[..., None] * per_block + jnp.arange(per_block, dtype=jnp.int32)).reshape(b, -1)
    sel_pages = jnp.take_along_axis(pt_flat.reshape(b, n_pages), page_pos, axis=1)
    o_b = _moba_decode_attention(q_b, pr["rows_b"].reshape(b, 1, -1), cache_moba, layer, sel_pages.reshape(-1),
                                 choice.reshape(-1), HEAD_DIM ** -0.5, past_len)[..., HEAD_DIM:]
    q_c = pr["q_c"].reshape(b, N_HEADS, 2, DIFF_QK_DIM).transpose(0, 2, 1, 3)
    z1 = jnp.zeros((b, N_HEADS, DIFF_QK_DIM), F32)
    z2 = jnp.zeros((b, N_HEADS, DIFF_V_DIM), F32)
    q_c = jnp.concatenate([jnp.concatenate([q_c[:, 0], z1, z2], -1), jnp.concatenate([z1, q_c[:, 1], z2], -1)], 1)
    o_c = _decode_attention("diff", q_c.astype(BF16), pr["rows_c"].reshape(b, 1, -1), cache_diff, layer, pt_flat,
                            DIFF_QK_DIM ** -0.5, lam=lam, g_norm=g_norm.reshape(N_HEADS, DIFF_V_DIM),
                            lam_init=lam_init)
    q_d = pr["q_d"].reshape(b, N_HEADS, HEAD_DIM)
    q_d = jnp.concatenate([q_d, jnp.zeros((b, N_HEADS, HEAD_DIM + N_HEADS), F32)], -1).astype(BF16)
    o_d = _decode_attention("fox", q_d, pr["rows_d"].reshape(b, 1, -1), cache_fox, layer, pt_flat, HEAD_DIM ** -0.5,
                            logf_new=pr["log_f"].reshape(b, N_HEADS, 1))[..., HEAD_DIM:]
    flat = lambda o: o.reshape(b, BRANCH_WIDTH)
    return jnp.stack([flat(o_a), flat(o_b), flat(o_c), flat(o_d)])


def kernel(x_prompt, x_sample, cache_mla, cache_moba, cache_diff, cache_fox, cache_mem_k, cache_mem_v, page_table, mem_prompt, ln_in_g, ln_in_b, ln_g, ln_b, w_in, mla_g_q, mla_g_kv, mla_w_q_up, mla_w_uk, mla_w_uv, diff_lambda, diff_g_norm, fox_b_f, w_branch, w_gate, b_gate, w_out, mem_w_q, mem_w_k, mem_w_v, mem_w_o, router_w, router_b, moe_w1, moe_w3, moe_w2):
    bp, tp, d = x_prompt.shape
    bs, ts, _ = x_sample.shape
    assert ts == 1 and tp % ATTN_BLOCK == 0 and tp // MOBA_BLOCK >= MOBA_TOPK
    depth = w_in.shape[0]
    n_pages = page_table.shape[1]
    assert n_pages % DECODE_PAGES == 0 and (n_pages * PAGE_SIZE) % MOBA_BLOCK == 0
    past_len = n_pages * PAGE_SIZE
    alpha = (2 * depth) ** 0.25
    pos_p = jnp.arange(tp)
    pos_s = past_len + jnp.arange(ts)
    pt_flat = page_table.reshape(-1)
    n_mem = mem_prompt.shape[1]

    router_w_p = jnp.pad(router_w, ((0, 0), (0, LANES - N_EXPERTS))).astype(BF16)
    router_b_p = jnp.pad(router_b.astype(F32), (0, LANES - N_EXPERTS)).reshape(1, LANES)
    mem_prompt_bf16 = mem_prompt.reshape(bp * n_mem, d).astype(BF16)
    cache_mla_t = cache_mla.transpose(0, 1, 3, 2)
    cache_fox_t = cache_fox.transpose(0, 1, 3, 2)

    xp, xp_b = _layer_norm(x_prompt.reshape(bp * tp, d), ln_in_g, ln_in_b)
    xs, xs_b = _layer_norm(x_sample.reshape(bs * ts, d), ln_in_g, ln_in_b)

    outs_p, outs_s, mem_ks, mem_vs = [], [], [], []
    for l in range(depth):
        lam_init = 0.8 - 0.6 * math.exp(-0.3 * l)
        lam = _diff_lambda(diff_lambda[l], lam_init)
        wl = _prep_weights(l, w_in, mla_w_q_up, mla_w_uk, mla_w_uv, w_branch, w_gate, b_gate, w_out, mem_w_q,
                           mem_w_k, mem_w_v, mem_w_o, moe_w1, moe_w3, moe_w2)
        mk = _matmul(mem_prompt_bf16, wl["mem_w_k"]).reshape(bp, n_mem, MEM_WIDTH)
        mv = _matmul(mem_prompt_bf16, wl["mem_w_v"]).reshape(bp, n_mem, MEM_WIDTH)

        pr = _project_inputs(xp_b, bp, tp, pos_p, wl, mla_g_q[l], mla_g_kv[l], fox_b_f[l])
        branches = _prompt_mixers(pr, bp, tp, wl, lam, diff_g_norm[l], lam_init)
        xp, xp_b = _finish_layer(xp, xp_b, branches, mk, mv, bp, tp, wl, ln_g[l], ln_b[l], router_w_p, router_b_p, alpha)
        outs_p.append(pr)

        ps = _project_inputs(xs_b, bs, ts, pos_s, wl, mla_g_q[l], mla_g_kv[l], fox_b_f[l])
        branches = _sample_mixers(ps, bs, wl, lam, diff_g_norm[l], lam_init, l,
                                  (cache_mla_t, cache_moba, cache_diff, cache_fox_t), pt_flat, n_pages)
        xs, xs_b = _finish_layer(xs, xs_b, branches, cache_mem_k[l].reshape(bs, n_mem, MEM_WIDTH),
                                 cache_mem_v[l].reshape(bs, n_mem, MEM_WIDTH), bs, ts, wl, ln_g[l], ln_b[l],
                                 router_w_p, router_b_p, alpha)
        outs_s.append(ps)
        mem_ks.append(mk.reshape(bp, n_mem, MEM_HEADS, MEM_HEAD_DIM))
        mem_vs.append(mv.reshape(bp, n_mem, MEM_HEADS, MEM_HEAD_DIM))

    rows = lambda outs, name, b, t: jnp.stack([o[name].reshape(b, t, -1) for o in outs])
    return (xp.reshape(bp, tp, d), xs.reshape(bs, ts, d),
            rows(outs_p, "rows_a", bp, tp), rows(outs_p, "rows_b", bp, tp),
            rows(outs_p, "rows_c", bp, tp), rows(outs_p, "rows_d", bp, tp),
            jnp.stack(mem_ks), jnp.stack(mem_vs),
            rows(outs_s, "rows_a", bs, ts), rows(outs_s, "rows_b", bs, ts),
            rows(outs_s, "rows_c", bs, ts), rows(outs_s, "rows_d", bs, ts))
```

```python
import functools
import math

import numpy as np
import jax
import jax.numpy as jnp
from jax import lax
from jax.experimental import pallas as pl
from jax.experimental.pallas import tpu as pltpu

F32 = jnp.float32
BF16 = jnp.bfloat16

D_MODEL = 2048
N_HEADS = 8
HEAD_DIM = 64
BRANCH_WIDTH = N_HEADS * HEAD_DIM
N_BRANCHES = 4
MLA_Q_LORA = 384
MLA_KV_LORA = 128
MLA_NOPE_DIM = HEAD_DIM
MLA_ROPE_DIM = 32
ROPE_THETA = 10000.0
MOBA_BLOCK = 256
MOBA_TOPK = 3
DIFF_QK_DIM = HEAD_DIM // 2
DIFF_V_DIM = HEAD_DIM
MEM_HEADS = 4
MEM_HEAD_DIM = 128
MEM_WIDTH = MEM_HEADS * MEM_HEAD_DIM
N_EXPERTS = 16
EXPERTS_PER_GROUP = 4
D_EXPERT = 512
PAGE_SIZE = 128
LN_EPS = 1e-5
RMS_EPS = 1e-6
NEG_INF = -1e30
BELOW_NEG_INF = -3e38

MLA_COLS = MLA_Q_LORA + MLA_KV_LORA + MLA_ROPE_DIM
MOBA_COLS = BRANCH_WIDTH + 2 * HEAD_DIM
DIFF_COLS = N_HEADS * 2 * DIFF_QK_DIM + 2 * DIFF_QK_DIM + DIFF_V_DIM
FOX_COLS = BRANCH_WIDTH + 2 * HEAD_DIM + N_HEADS
IN_COLS = MLA_COLS + MOBA_COLS + DIFF_COLS + FOX_COLS

LANES = 128
VMEM_LIMIT_BYTES = 56 * 1024 * 1024

ATTN_BLOCK = 512
LOG2E = 1.4426950408889634
DECODE_PAGES = 64

_NT = (((1,), (1,)), ((), ()))


def _cparams(*sem):
    return pltpu.CompilerParams(dimension_semantics=sem, vmem_limit_bytes=VMEM_LIMIT_BYTES)


def _tile(n, candidates):
    for c in candidates:
        if n % c == 0:
            return c
    return n


def _layer_norm_rows(y, g, b):
    mu = jnp.mean(y, -1, keepdims=True)
    yc = y - mu
    var = jnp.mean(yc * yc, -1, keepdims=True)
    return yc * lax.rsqrt(var + LN_EPS) * g + b


def _ln_body(x_ref, g_ref, b_ref, o_ref, ob_ref):
    y = _layer_norm_rows(x_ref[...], g_ref[...], b_ref[...])
    o_ref[...] = y
    ob_ref[...] = y.astype(BF16)


def _layer_norm(x, g, b):
    m, d = x.shape
    tm = _tile(m, (512, 256, 128))
    return pl.pallas_call(
        _ln_body,
        grid=(m // tm,),
        in_specs=[pl.BlockSpec((tm, d), lambda i: (i, 0)),
                  pl.BlockSpec((1, d), lambda i: (0, 0)),
                  pl.BlockSpec((1, d), lambda i: (0, 0))],
        out_specs=[pl.BlockSpec((tm, d), lambda i: (i, 0)),
                   pl.BlockSpec((tm, d), lambda i: (i, 0))],
        out_shape=[jax.ShapeDtypeStruct((m, d), F32), jax.ShapeDtypeStruct((m, d), BF16)],
        compiler_params=_cparams("parallel"),
        name="layer_norm",
    )(x, g.reshape(1, d), b.reshape(1, d))


def _mm_body(x_ref, w_ref, o_ref):
    o_ref[...] = jnp.dot(x_ref[...].astype(BF16), w_ref[...], preferred_element_type=F32).astype(o_ref.dtype)


def _mm_rms_body(x_ref, g_ref, w_ref, o_ref):
    x = x_ref[...]
    xn = x * lax.rsqrt(jnp.mean(x * x, -1, keepdims=True) + RMS_EPS) * g_ref[...]
    o_ref[...] = jnp.dot(xn.astype(BF16), w_ref[...], preferred_element_type=F32).astype(o_ref.dtype)


def _matmul(x, w, out_dtype=F32, rms_gain=None):
    m, k = x.shape
    n = w.shape[1]
    tm = _tile(m, (512, 256, 128))
    tn = _tile(n, (1024, 768, 640, 512, 384, 256, 128))
    x_spec = pl.BlockSpec((tm, k), lambda i, j: (i, 0))
    w_spec = pl.BlockSpec((k, tn), lambda i, j: (0, j))
    o_spec = pl.BlockSpec((tm, tn), lambda i, j: (i, j))
    if rms_gain is None:
        body, specs, args = _mm_body, [x_spec, w_spec], (x, w)
    else:
        body = _mm_rms_body
        specs = [x_spec, pl.BlockSpec((1, k), lambda i, j: (0, 0)), w_spec]
        args = (x, rms_gain.reshape(1, k), w)
    return pl.pallas_call(
        body, grid=(m // tm, n // tn), in_specs=specs, out_specs=o_spec,
        out_shape=jax.ShapeDtypeStruct((m, n), out_dtype),
        compiler_params=_cparams("parallel", "arbitrary"),
        name="matmul",
    )(*args)


def _bmm_body(x_ref, w_ref, o_ref):
    o_ref[...] = jnp.dot(x_ref[...], w_ref[...], preferred_element_type=F32).astype(o_ref.dtype)


def _head_matmul(x, w, out_dtype):
    h, m, k = x.shape
    n = w.shape[2]
    tm = _tile(m, (1024, 512, 256, 128))
    return pl.pallas_call(
        _bmm_body, grid=(h, m // tm),
        in_specs=[pl.BlockSpec((None, tm, k), lambda a, i: (a, i, 0)),
                  pl.BlockSpec((None, k, n), lambda a, i: (a, 0, 0))],
        out_specs=pl.BlockSpec((None, tm, n), lambda a, i: (a, i, 0)),
        out_shape=jax.ShapeDtypeStruct((h, m, n), out_dtype),
        compiler_params=_cparams("parallel", "arbitrary"),
        name="head_matmul",
    )(x, w)


def _mm_ln_body(x_ref, w_ref, r_ref, g_ref, b_ref, o_ref, ob_ref, *, alpha):
    y = alpha * r_ref[...] + jnp.dot(x_ref[...], w_ref[...], preferred_element_type=F32)
    y = _layer_norm_rows(y, g_ref[...], b_ref[...])
    o_ref[...] = y
    ob_ref[...] = y.astype(BF16)


def _matmul_residual_ln(x, w, resid, g, b, alpha):
    m, k = x.shape
    d = w.shape[1]
    tm = _tile(m, (256, 128))
    row = lambda i: (i, 0)
    fixed = lambda i: (0, 0)
    return pl.pallas_call(
        functools.partial(_mm_ln_body, alpha=alpha),
        grid=(m // tm,),
        in_specs=[pl.BlockSpec((tm, k), row), pl.BlockSpec((k, d), fixed), pl.BlockSpec((tm, d), row),
                  pl.BlockSpec((1, d), fixed), pl.BlockSpec((1, d), fixed)],
        out_specs=[pl.BlockSpec((tm, d), row), pl.BlockSpec((tm, d), row)],
        out_shape=[jax.ShapeDtypeStruct((m, d), F32), jax.ShapeDtypeStruct((m, d), BF16)],
        compiler_params=_cparams("parallel"),
        name="matmul_residual_ln",
    )(x, w, resid, g.reshape(1, d), b.reshape(1, d))


def _merge_body(x_ref, wg_ref, bg_ref, o_ref, wb_ref, out_ref, acc_ref):
    n = pl.program_id(2)

    @pl.when(n == 0)
    def _():
        acc_ref[...] = jnp.zeros_like(acc_ref)

    gate = jax.nn.sigmoid(jnp.dot(x_ref[...], wg_ref[...], preferred_element_type=F32) + bg_ref[...])
    proj = jnp.dot(o_ref[...], wb_ref[...], preferred_element_type=F32)
    acc_ref[...] += gate * proj

    @pl.when(n == N_BRANCHES - 1)
    def _():
        out_ref[...] = acc_ref[...].astype(out_ref.dtype)


def _gated_merge(x, w_gate, b_gate, branches, w_branch):
    m, d = x.shape
    c = branches.shape[2]
    tm = _tile(m, (1024, 512, 256, 128))
    tn = 512
    nj = d // tn
    return pl.pallas_call(
        _merge_body,
        grid=(m // tm, nj, N_BRANCHES),
        in_specs=[pl.BlockSpec((tm, d), lambda i, j, n: (i, 0)),
                  pl.BlockSpec((d, tn), lambda i, j, n: (0, n * nj + j)),
                  pl.BlockSpec((1, tn), lambda i, j, n: (0, n * nj + j)),
                  pl.BlockSpec((None, tm, c), lambda i, j, n: (n, i, 0)),
                  pl.BlockSpec((None, c, tn), lambda i, j, n: (n, 0, j))],
        out_specs=pl.BlockSpec((tm, tn), lambda i, j, n: (i, j)),
        out_shape=jax.ShapeDtypeStruct((m, d), BF16),
        scratch_shapes=[pltpu.VMEM((tm, tn), F32)],
        compiler_params=_cparams("parallel", "arbitrary", "arbitrary"),
        name="gated_merge",
    )(x, w_gate, b_gate, branches, w_branch)


def _mem_attn_body(q_ref, k_ref, v_ref, o_ref):
    scale = MEM_HEAD_DIM ** -0.5
    outs = []
    for h in range(MEM_HEADS):
        sl = slice(h * MEM_HEAD_DIM, (h + 1) * MEM_HEAD_DIM)
        n_mem = k_ref.shape[0] // MEM_HEADS
        k = k_ref[pl.ds(h, n_mem, stride=MEM_HEADS), :].astype(BF16)
        v = v_ref[pl.ds(h, n_mem, stride=MEM_HEADS), :].astype(BF16)
        s = lax.dot_general(q_ref[:, sl], k, _NT, preferred_element_type=F32) * scale
        m = jnp.max(s, -1, keepdims=True)
        e = jnp.exp(s - m)
        p = e / jnp.sum(e, -1, keepdims=True)
        outs.append(jnp.dot(p.astype(BF16), v, preferred_element_type=F32))
    o_ref[...] = jnp.concatenate(outs, axis=-1).astype(o_ref.dtype)


def _mem_attention(q, mem_k, mem_v, layer):
    b, t, w = q.shape
    rows, hd = mem_k.shape[2:]
    tq = _tile(t, (512, 256, 128))
    return pl.pallas_call(
        _mem_attn_body, grid=(b, t // tq),
        in_specs=[pl.BlockSpec((None, tq, w), lambda a, i: (a, i, 0)),
                  pl.BlockSpec((None, None, rows, hd), lambda a, i: (layer, a, 0, 0)),
                  pl.BlockSpec((None, None, rows, hd), lambda a, i: (layer, a, 0, 0))],
        out_specs=pl.BlockSpec((None, tq, w), lambda a, i: (a, i, 0)),
        out_shape=jax.ShapeDtypeStruct((b, t, w), BF16),
        compiler_params=_cparams("parallel", "arbitrary"),
        name="mem_attention",
    )(q, mem_k, mem_v)


def _router_body(x_ref, w_ref, b_ref, gate_ref):
    logits = jnp.dot(x_ref[...], w_ref[...], preferred_element_type=F32)
    scores = jax.nn.sigmoid(logits)
    sel = scores + b_ref[...]
    lane = lax.broadcasted_iota(jnp.int32, sel.shape, 1)
    in_grp = lane % EXPERTS_PER_GROUP
    grp = lane // EXPERTS_PER_GROUP
    real = lane < N_EXPERTS

    rank = jnp.zeros(sel.shape, jnp.int32)
    for d in range(1, EXPERTS_PER_GROUP):
        lower = pltpu.roll(sel, d, 1)
        upper = pltpu.roll(sel, LANES - d, 1)
        rank += jnp.where((in_grp >= d) & (lower >= sel), 1, 0)
        rank += jnp.where((in_grp + d < EXPERTS_PER_GROUP) & (upper > sel), 1, 0)
    top2 = rank < 2

    v = jnp.where(top2, sel, 0.0)
    gs = v
    for d in range(1, EXPERTS_PER_GROUP):
        gs += jnp.where(in_grp >= d, pltpu.roll(v, d, 1), 0.0)
        gs += jnp.where(in_grp + d < EXPERTS_PER_GROUP, pltpu.roll(v, LANES - d, 1), 0.0)

    n_groups = N_EXPERTS // EXPERTS_PER_GROUP
    beaten = jnp.zeros(sel.shape, jnp.int32)
    for d in range(1, n_groups):
        lower = pltpu.roll(gs, d * EXPERTS_PER_GROUP, 1)
        upper = pltpu.roll(gs, LANES - d * EXPERTS_PER_GROUP, 1)
        beaten += jnp.where((grp >= d) & (lower >= gs), 1, 0)
        beaten += jnp.where((grp + d < n_groups) & (upper > gs), 1, 0)
    chosen = top2 & (beaten == 0) & real

    w_sel = jnp.where(chosen, scores, 0.0)
    gate_ref[...] = w_sel / jnp.sum(w_sel, -1, keepdims=True)


def _router(x, router_w, router_b):
    m, d = x.shape
    tm = _tile(m, (512, 256, 128))
    return pl.pallas_call(
        _router_body, grid=(m // tm,),
        in_specs=[pl.BlockSpec((tm, d), lambda i: (i, 0)),
                  pl.BlockSpec((d, LANES), lambda i: (0, 0)),
                  pl.BlockSpec((1, LANES), lambda i: (0, 0))],
        out_specs=pl.BlockSpec((tm, LANES), lambda i: (i, 0)),
        out_shape=jax.ShapeDtypeStruct((m, LANES), F32),
        compiler_params=_cparams("parallel"),
        name="moe_router",
    )(x, router_w, router_b)


def _moe_body(x_ref, gate_ref, w1_ref, w3_ref, w2_ref, r_ref, g_ref, b_ref, o_ref, ob_ref, acc_ref, *, alpha):
    e = pl.program_id(1)

    @pl.when(e == 0)
    def _():
        acc_ref[...] = jnp.zeros_like(acc_ref)

    x = x_ref[...]
    h1 = jnp.dot(x, w1_ref[...], preferred_element_type=F32)
    h3 = jnp.dot(x, w3_ref[...], preferred_element_type=F32)
    gate = gate_ref[...]
    lane = lax.broadcasted_iota(jnp.int32, gate.shape, 1)
    ge = jnp.sum(jnp.where(lane == e, gate, 0.0), -1, keepdims=True)
    hdn = jax.nn.silu(h1) * h3 * ge
    acc_ref[...] += jnp.dot(hdn.astype(BF16), w2_ref[...], preferred_element_type=F32)

    @pl.when(e == N_EXPERTS - 1)
    def _():
        y = _layer_norm_rows(alpha * r_ref[...] + acc_ref[...], g_ref[...], b_ref[...])
        o_ref[...] = y
        ob_ref[...] = y.astype(BF16)


def _moe_residual_ln(x, gate, w1, w3, w2, resid, g, b, alpha):
    m, d = x.shape
    f = w1.shape[2]
    tm = _tile(m, (512, 256, 128))
    row = lambda i, e: (i, 0)
    fixed = lambda i, e: (0, 0)
    return pl.pallas_call(
        functools.partial(_moe_body, alpha=alpha),
        grid=(m // tm, N_EXPERTS),
        in_specs=[pl.BlockSpec((tm, d), row), pl.BlockSpec((tm, LANES), row),
                  pl.BlockSpec((None, d, f), lambda i, e: (e, 0, 0)),
                  pl.BlockSpec((None, d, f), lambda i, e: (e, 0, 0)),
                  pl.BlockSpec((None, f, d), lambda i, e: (e, 0, 0)),
                  pl.BlockSpec((tm, d), row), pl.BlockSpec((1, d), fixed), pl.BlockSpec((1, d), fixed)],
        out_specs=[pl.BlockSpec((tm, d), row), pl.BlockSpec((tm, d), row)],
        out_shape=[jax.ShapeDtypeStruct((m, d), F32), jax.ShapeDtypeStruct((m, d), BF16)],
        scratch_shapes=[pltpu.VMEM((tm, d), F32)],
        compiler_params=_cparams("parallel", "arbitrary"),
        name="moe_experts",
    )(x, gate, w1, w3, w2, resid, g.reshape(1, d), b.reshape(1, d))


def _lane_prefix_sum(x):
    lane = lax.broadcasted_iota(jnp.int32, x.shape, 1)
    shift = 1
    while shift < LANES:
        x = x + jnp.where(lane >= shift, pltpu.roll(x, shift, 1), 0.0)
        shift *= 2
    return x


def _cumsum_body(*refs, pieces, n_prefetch):
    x_refs, o_ref, carry_ref = refs[n_prefetch:-2], refs[-2], refs[-1]

    @pl.when(pl.program_id(1) == 0)
    def _():
        carry_ref[...] = jnp.zeros_like(carry_ref)

    carry = carry_ref[...]
    for p in range(pieces):
        x = x_refs[p][...] if len(x_refs) > 1 else x_refs[0][:, p * LANES:(p + 1) * LANES]
        c = _lane_prefix_sum(x) + carry
        o_ref[:, p * LANES:(p + 1) * LANES] = c
        carry = c[:, LANES - 1:LANES]
    carry_ref[...] = carry


def _cumsum_rows(x):
    b, r, l = x.shape
    pieces = _tile(l // LANES, (16, 8, 4, 2, 1))
    w = pieces * LANES
    return pl.pallas_call(
        functools.partial(_cumsum_body, pieces=pieces, n_prefetch=0),
        grid=(b, l // w),
        in_specs=[pl.BlockSpec((None, r, w), lambda a, j: (a, 0, j))],
        out_specs=pl.BlockSpec((None, r, w), lambda a, j: (a, 0, j)),
        out_shape=jax.ShapeDtypeStruct((b, r, l), F32),
        scratch_shapes=[pltpu.VMEM((r, 1), F32)],
        compiler_params=_cparams("parallel", "arbitrary"),
        name="cumsum_rows",
    )(x)


def _block_mean_body(k_ref, o_ref):
    o_ref[...] = jnp.sum(k_ref[...], 0, keepdims=True) * (1.0 / MOBA_BLOCK)


def _moba_block_means(k):
    b, l, d = k.shape
    nb = l // MOBA_BLOCK
    out = pl.pallas_call(
        _block_mean_body, grid=(b, nb),
        in_specs=[pl.BlockSpec((None, MOBA_BLOCK, d), lambda a, n: (a, n, 0))],
        out_specs=pl.BlockSpec((None, None, 1, d), lambda a, n: (a, n, 0, 0)),
        out_shape=jax.ShapeDtypeStruct((b, nb, 1, d), F32),
        compiler_params=_cparams("parallel", "arbitrary"),
        name="moba_block_means",
    )(k)
    return out.reshape(b, nb, d)


def _paged_block_mean_body(pt_ref, *refs):
    page_refs, o_ref = refs[:-1], refs[-1]
    per_block = MOBA_BLOCK // PAGE_SIZE
    for n in range(len(page_refs) // per_block):
        tot = jnp.sum(page_refs[n * per_block][...], 0, keepdims=True)
        for t in range(1, per_block):
            tot = tot + jnp.sum(page_refs[n * per_block + t][...], 0, keepdims=True)
        o_ref[n] = tot * (1.0 / MOBA_BLOCK)


def _moba_block_means_paged(cache, layer, page_table_flat, n_seq, n_pages):
    w = cache.shape[3]
    pp = DECODE_PAGES
    bps = pp * PAGE_SIZE // MOBA_BLOCK
    nb = n_pages * PAGE_SIZE // MOBA_BLOCK
    specs = [pl.BlockSpec((None, None, PAGE_SIZE, w),
                          functools.partial(lambda a, j, pt, p: (layer, pt[a * n_pages + j * pp + p], 0, 0), p=p))
             for p in range(pp)]
    out = pl.pallas_call(
        _paged_block_mean_body,
        grid_spec=pltpu.PrefetchScalarGridSpec(
            num_scalar_prefetch=1, grid=(n_seq, n_pages // pp),
            in_specs=specs,
            out_specs=pl.BlockSpec((None, bps, 1, w), lambda a, j, pt: (a, j, 0, 0))),
        out_shape=jax.ShapeDtypeStruct((n_seq, nb, 1, w), F32),
        compiler_params=_cparams("parallel", "arbitrary"),
        name="moba_block_means_paged",
    )(page_table_flat, *([cache] * pp))
    return out.reshape(n_seq, nb, w)


def _top_blocks(gate, limit):
    lane = lax.broadcasted_iota(jnp.int32, gate.shape, gate.ndim - 1)
    cur = jnp.where(lane < limit, gate, NEG_INF)
    picked = jnp.zeros(gate.shape, jnp.bool_)
    for _ in range(MOBA_TOPK):
        cur = jnp.where(picked, BELOW_NEG_INF, cur)
        best = jnp.max(cur, -1, keepdims=True)
        first = jnp.min(jnp.where(cur == best, lane, LANES), -1, keepdims=True)
        picked = picked | (lane == first)
    return picked & (lane < limit)


def _flash_body(qi_ref, kj_ref, *refs, mode, n_heads, blk, scale, lam_init):
    q_ref, k_ref, v_ref = refs[:3]
    refs = refs[3:]
    if mode == "fox":
        ck_ref = refs[0]
        refs = refs[1:]
    if mode in ("moba", "diff"):
        slope_ref = refs[0]
        refs = refs[1:]
    if mode == "moba":
        kmean_ref = refs[0]
        refs = refs[1:]
    if mode == "diff":
        lam_ref, gn_ref = refs[:2]
        refs = refs[2:]
    o_ref, m_ref, acc_ref = refs[:3]
    sel_ref = refs[3] if mode == "moba" else None

    t = pl.program_id(1)
    i = qi_ref[t]
    j = kj_ref[t]
    sub = blk // MOBA_BLOCK
    dk = q_ref.shape[-1]
    dv = o_ref.shape[-1]
    rows = n_heads * blk

    row = lax.broadcasted_iota(jnp.int32, (1, blk, blk), 1)
    col = lax.broadcasted_iota(jnp.int32, (1, blk, blk), 2)

    @pl.when(j == 0)
    def _():
        m_ref[...] = jnp.full_like(m_ref, NEG_INF)
        acc_ref[...] = jnp.zeros_like(acc_ref)
        if mode == "moba":
            own = (i * blk + lax.broadcasted_iota(jnp.int32, (1, blk, 1), 1)) // MOBA_BLOCK
            gate = lax.dot_general(q_ref[...].reshape(rows, dk), kmean_ref[...], _NT,
                                   preferred_element_type=F32).reshape(n_heads, blk, LANES)
            sel_ref[...] = jnp.where(_top_blocks(gate, own), 1.0, 0.0)

    def tile(diag):
        s = lax.dot_general(q_ref[...].reshape(rows, dk), k_ref[...], _NT, preferred_element_type=F32)
        s = s.reshape(n_heads, blk, blk) * (scale * LOG2E)
        if mode in ("moba", "diff"):
            key_off = ((j - i) * blk + lax.broadcasted_iota(jnp.int32, (1, 1, blk), 2)).astype(F32)
            s = s + (slope_ref[...] * LOG2E) * key_off
        if mode == "fox":
            s = s - ck_ref[...] * LOG2E
        if mode == "moba":
            sel = sel_ref[...]
            lane = lax.broadcasted_iota(jnp.int32, (1, 1, LANES), 2)
            picked = None
            for u in range(sub):
                on = jnp.sum(jnp.where(lane == j * sub + u, sel, 0.0), -1, keepdims=True) > 0.0
                if sub > 1:
                    on = on & (col // MOBA_BLOCK == u)
                picked = on if picked is None else picked | on
            if diag:
                own_blk = row // MOBA_BLOCK
                key_blk = col // MOBA_BLOCK
                valid = ((key_blk == own_blk) & (row >= col)) | ((key_blk < own_blk) & picked)
            else:
                valid = picked
            s = jnp.where(valid, s, NEG_INF)
        elif diag:
            s = jnp.where(row >= col, s, NEG_INF)
        m_prev = m_ref[...]
        m_new = jnp.maximum(m_prev, jnp.max(s, -1, keepdims=True))
        p = jnp.exp2(s - m_new)
        pv = jnp.dot(p.astype(BF16).reshape(rows, blk), v_ref[...], preferred_element_type=F32)
        acc_ref[...] = jnp.exp2(m_prev - m_new) * acc_ref[...] + pv.reshape(acc_ref.shape)
        m_ref[...] = m_new

    @pl.when(j < i)
    def _():
        tile(False)

    @pl.when(j == i)
    def _():
        tile(True)
        acc = acc_ref[...]
        out = acc[..., :dv] / acc[..., dv:dv + 1]
        if mode == "diff":
            a = out[:N_HEADS] - lam_ref[0, 0] * out[N_HEADS:]
            a = a * lax.rsqrt(jnp.mean(a * a, -1, keepdims=True) + RMS_EPS) * gn_ref[...]
            out = a * (1.0 - lam_init)
        o_ref[...] = out.astype(o_ref.dtype)


def _flash_attention(mode, q, k, v, scale, *, c=None, kmean=None, lam=None, g_norm=None, lam_init=0.0):
    nh, b, t, dk = q.shape
    dv = v.shape[2]
    blk = ATTN_BLOCK
    nq = t // blk
    pairs = [(i, j) for i in range(nq) for j in range(i + 1)]
    qi = jnp.asarray(np.array([p[0] for p in pairs], np.int32))
    kj = jnp.asarray(np.array([p[1] for p in pairs], np.int32))
    h_out = N_HEADS if mode == "diff" else nh
    dva = -(-(dv + 1) // LANES) * LANES
    v = jnp.concatenate([v, jnp.ones((b, t, 1), BF16), jnp.zeros((b, t, dva - dv - 1), BF16)], -1)

    in_specs = [pl.BlockSpec((nh, None, blk, dk), lambda a, s, qi, kj: (0, a, qi[s], 0)),
                pl.BlockSpec((None, blk, dk), lambda a, s, qi, kj: (a, kj[s], 0)),
                pl.BlockSpec((None, blk, dva), lambda a, s, qi, kj: (a, kj[s], 0))]
    args = [q, k, v]
    scratch = [pltpu.VMEM((nh, blk, 1), F32), pltpu.VMEM((nh, blk, dva), F32)]
    if mode == "fox":
        in_specs.append(pl.BlockSpec((None, nh, 1, blk), lambda a, s, qi, kj: (a, 0, 0, kj[s])))
        args.append(c)
    if mode in ("moba", "diff"):
        in_specs.append(pl.BlockSpec((nh, 1, 1), lambda a, s, qi, kj: (0, 0, 0)))
        args.append(jnp.asarray(np.array([2.0 ** -(h % N_HEADS + 1) for h in range(nh)], np.float32).reshape(nh, 1, 1)))
    if mode == "moba":
        in_specs.append(pl.BlockSpec((None, LANES, dk), lambda a, s, qi, kj: (a, 0, 0)))
        args.append(kmean)
        scratch.append(pltpu.VMEM((nh, blk, LANES), F32))
    if mode == "diff":
        in_specs += [pl.BlockSpec(memory_space=pltpu.SMEM),
                     pl.BlockSpec((N_HEADS, 1, dv), lambda a, s, qi, kj: (0, 0, 0))]
        args += [lam.reshape(1, 1), g_norm]

    return pl.pallas_call(
        functools.partial(_flash_body, mode=mode, n_heads=nh, blk=blk, scale=scale, lam_init=lam_init),
        grid_spec=pltpu.PrefetchScalarGridSpec(
            num_scalar_prefetch=2, grid=(b, len(pairs)), in_specs=in_specs,
            out_specs=pl.BlockSpec((h_out, None, blk, dv), lambda a, s, qi, kj: (0, a, qi[s], 0)),
            scratch_shapes=scratch),
        out_shape=jax.ShapeDtypeStruct((h_out, b, t, dv), BF16),
        compiler_params=_cparams("parallel", "arbitrary"),
        name="flash_" + mode,
    )(qi, kj, *args)


def _page_prefix_sums(x):
    hi = x.astype(BF16)
    rest = x - hi.astype(F32)
    mid = rest.astype(BF16)
    low = (rest - mid.astype(F32)).astype(BF16)
    src = lax.broadcasted_iota(jnp.int32, (LANES, LANES), 0)
    dst = lax.broadcasted_iota(jnp.int32, (LANES, LANES), 1)
    tri = jnp.where(src <= dst, 1.0, 0.0).astype(BF16)
    dot = lambda a: jnp.dot(a, tri, preferred_element_type=F32)
    return dot(hi) + dot(mid) + dot(low)


def _decode_body(pt_ref, *refs, mode, scale, past_len, lam_init):
    pp = DECODE_PAGES
    keys_on_lanes = mode in ("mla", "fox")
    q_ref, new_ref = refs[:2]
    page_refs = refs[2:2 + pp]
    refs = refs[2 + pp:]
    if mode == "fox":
        logf_new_ref = refs[0]
        refs = refs[1:]
    if mode == "diff":
        lam_ref, gn_ref, slope_ref = refs[:3]
        refs = refs[3:]
    o_ref, m_ref, l_ref, acc_ref, keys_ref = refs[:5]

    j = pl.program_id(1)
    span = pp * PAGE_SIZE

    @pl.when(j == 0)
    def _():
        m_ref[...] = jnp.full_like(m_ref, NEG_INF)
        l_ref[...] = jnp.zeros_like(l_ref)
        acc_ref[...] = jnp.zeros_like(acc_ref)
        if mode == "fox":
            refs[5][...] = jnp.zeros_like(refs[5])

    q = q_ref[...]
    for p in range(pp):
        if keys_on_lanes:
            keys_ref[:, p * PAGE_SIZE:(p + 1) * PAGE_SIZE] = page_refs[p][...].astype(BF16)
        else:
            keys_ref[p * PAGE_SIZE:(p + 1) * PAGE_SIZE, :] = page_refs[p][...].astype(BF16)
    if keys_on_lanes:
        s = jnp.dot(q, keys_ref[...], preferred_element_type=F32) * scale
    else:
        s = lax.dot_general(q, keys_ref[...], _NT, preferred_element_type=F32) * scale
    if mode == "diff":
        dist = (past_len - j * span - lax.broadcasted_iota(jnp.int32, (1, span), 1)).astype(F32)
        s = s - slope_ref[...] * dist
    if mode == "fox":
        carry_ref = refs[5]
        logf = jnp.concatenate([page_refs[p][2 * HEAD_DIM:2 * HEAD_DIM + N_HEADS, :] for p in range(pp)], axis=0)
        within = _page_prefix_sums(logf)
        carry = carry_ref[...]
        pieces = []
        for p in range(pp):
            w = within[p * N_HEADS:(p + 1) * N_HEADS]
            pieces.append(w + carry)
            carry = carry + w[:, LANES - 1:LANES]
        carry_ref[...] = carry
        s = s - jnp.concatenate(pieces, axis=1)
    m_prev = m_ref[...]
    m_new = jnp.maximum(m_prev, jnp.max(s, -1, keepdims=True))
    alpha = jnp.exp(m_prev - m_new)
    p = jnp.exp(s - m_new)
    l_ref[...] = alpha * l_ref[...] + jnp.sum(p, -1, keepdims=True)
    pb = p.astype(BF16)
    if keys_on_lanes:
        pv = lax.dot_general(pb, keys_ref[:LANES, :], _NT, preferred_element_type=F32)
    else:
        pv = jnp.dot(pb, keys_ref[...], preferred_element_type=F32)
    acc_ref[...] = alpha * acc_ref[...] + pv
    m_ref[...] = m_new

    @pl.when(j == pl.num_programs(1) - 1)
    def _():
        new = new_ref[...].astype(BF16).astype(F32)
        s_new = jnp.sum(q.astype(F32) * new, -1, keepdims=True) * scale
        if mode == "fox":
            s_new = s_new - (refs[5][...] + logf_new_ref[...])
        m_old = m_ref[...]
        m_fin = jnp.maximum(m_old, s_new)
        a_old = jnp.exp(m_old - m_fin)
        p_new = jnp.exp(s_new - m_fin)
        l_fin = a_old * l_ref[...] + p_new
        acc_fin = a_old * acc_ref[...] + p_new.astype(BF16).astype(F32) * new[:, :LANES]
        out = acc_fin / l_fin
        if mode == "diff":
            a = out[:N_HEADS, DIFF_V_DIM:] - lam_ref[0, 0] * out[N_HEADS:, DIFF_V_DIM:]
            a = a * lax.rsqrt(jnp.mean(a * a, -1, keepdims=True) + RMS_EPS) * gn_ref[...]
            o_ref[...] = (a * (1.0 - lam_init)).astype(o_ref.dtype)
        else:
            o_ref[...] = out.astype(o_ref.dtype)


def _decode_attention(mode, q, new_rows, cache, layer, page_table_flat, scale, *, logf_new=None,
                      lam=None, g_norm=None, lam_init=0.0):
    b, n_rows, width = q.shape
    n_pages = page_table_flat.shape[0] // b
    pp = DECODE_PAGES
    past_len = n_pages * PAGE_SIZE
    page_block = (None, None, width, PAGE_SIZE) if mode in ("mla", "fox") else (None, None, PAGE_SIZE, width)
    in_specs = [pl.BlockSpec((None, n_rows, width), lambda a, j, pt: (a, 0, 0)),
                pl.BlockSpec((None, 1, width), lambda a, j, pt: (a, 0, 0))]
    in_specs += [pl.BlockSpec(page_block,
                              functools.partial(lambda a, j, pt, p: (layer, pt[a * n_pages + j * pp + p], 0, 0), p=p))
                 for p in range(pp)]
    args = [q, new_rows] + [cache] * pp
    out_rows, out_w = n_rows, LANES
    keys_shape = (width, pp * PAGE_SIZE) if mode in ("mla", "fox") else (pp * PAGE_SIZE, width)
    scratch = [pltpu.VMEM((n_rows, 1), F32), pltpu.VMEM((n_rows, 1), F32), pltpu.VMEM((n_rows, LANES), F32),
               pltpu.VMEM(keys_shape, BF16)]
    if mode == "fox":
        in_specs.append(pl.BlockSpec((None, n_rows, 1), lambda a, j, pt: (a, 0, 0)))
        args.append(logf_new)
        scratch.append(pltpu.VMEM((n_rows, 1), F32))
    if mode == "diff":
        in_specs += [pl.BlockSpec(memory_space=pltpu.SMEM),
                     pl.BlockSpec((N_HEADS, DIFF_V_DIM), lambda a, j, pt: (0, 0)),
                     pl.BlockSpec((n_rows, 1), lambda a, j, pt: (0, 0))]
        slopes = jnp.asarray(np.array([2.0 ** -(r % N_HEADS + 1) for r in range(n_rows)], np.float32).reshape(n_rows, 1))
        args += [lam.reshape(1, 1), g_norm, slopes]
        out_rows, out_w = N_HEADS, DIFF_V_DIM
    return pl.pallas_call(
        functools.partial(_decode_body, mode=mode, scale=scale, past_len=past_len, lam_init=lam_init),
        grid_spec=pltpu.PrefetchScalarGridSpec(
            num_scalar_prefetch=1, grid=(b, n_pages // pp), in_specs=in_specs,
            out_specs=pl.BlockSpec((None, out_rows, out_w), lambda a, j, pt: (a, 0, 0)),
            scratch_shapes=scratch),
        out_shape=jax.ShapeDtypeStruct((b, out_rows, out_w), BF16),
        compiler_params=_cparams("parallel", "arbitrary"),
        name="decode_" + mode,
    )(page_table_flat, *args)


def _moba_pick_body(q_ref, km_ref, o_ref, *, n_blocks):
    gate = lax.dot_general(q_ref[...], km_ref[...].astype(BF16), _NT, preferred_element_type=F32)
    lane = lax.broadcasted_iota(jnp.int32, gate.shape, 1)
    cur = jnp.where(lane < n_blocks, gate, BELOW_NEG_INF)
    out = jnp.zeros(gate.shape, jnp.int32)
    for n in range(MOBA_TOPK):
        best = jnp.max(cur, -1, keepdims=True)
        first = jnp.min(jnp.where(cur == best, lane, LANES), -1, keepdims=True)
        out = jnp.where(lane == n, first, out)
        cur = jnp.where(lane == first, BELOW_NEG_INF, cur)
    o_ref[...] = out


def _moba_pick(q, kmean_padded, n_blocks):
    b, r, w = q.shape
    return pl.pallas_call(
        functools.partial(_moba_pick_body, n_blocks=n_blocks), grid=(b,),
        in_specs=[pl.BlockSpec((None, r, w), lambda a: (a, 0, 0)),
                  pl.BlockSpec((None, LANES, w), lambda a: (a, 0, 0))],
        out_specs=pl.BlockSpec((None, r, LANES), lambda a: (a, 0, 0)),
        out_shape=jax.ShapeDtypeStruct((b, r, LANES), jnp.int32),
        compiler_params=_cparams("parallel"),
        name="moba_pick",
    )(q, kmean_padded)


def _moba_decode_body(pg_ref, blk_ref, q_ref, new_ref, *refs, scale, past_len):
    page_refs, o_ref = refs[:-1], refs[-1]
    a = pl.program_id(0)
    per_block = MOBA_BLOCK // PAGE_SIZE
    per_head = MOBA_TOPK * per_block
    n_sel = N_HEADS * per_head
    q = q_ref[...]
    row = lax.broadcasted_iota(jnp.int32, (N_HEADS, PAGE_SIZE), 0)
    lane = lax.broadcasted_iota(jnp.int32, (N_HEADS, PAGE_SIZE), 1)
    slope = jnp.exp2(-(row + 1).astype(F32))
    keys, scores = [], []
    for u in range(n_sel):
        h, rest = divmod(u, per_head)
        n, half = divmod(rest, per_block)
        kp = page_refs[u][...].astype(BF16)
        keys.append(kp)
        s = lax.dot_general(q, kp, _NT, preferred_element_type=F32) * scale
        first_pos = blk_ref[a * N_HEADS * MOBA_TOPK + h * MOBA_TOPK + n] * MOBA_BLOCK + half * PAGE_SIZE
        dist = (past_len - first_pos - lane).astype(F32)
        scores.append(jnp.where(row == h, s - slope * dist, NEG_INF))
    s = jnp.concatenate(scores, axis=1)
    new = new_ref[...].astype(BF16).astype(F32)
    s_new = jnp.sum(q.astype(F32) * new, -1, keepdims=True) * scale
    m = jnp.maximum(jnp.max(s, -1, keepdims=True), s_new)
    p = jnp.exp(s - m)
    p_new = jnp.exp(s_new - m)
    denom = jnp.sum(p, -1, keepdims=True) + p_new
    pb = p.astype(BF16)
    acc = p_new.astype(BF16).astype(F32) * new
    for u in range(n_sel):
        acc = acc + jnp.dot(pb[:, u * PAGE_SIZE:(u + 1) * PAGE_SIZE], keys[u], preferred_element_type=F32)
    o_ref[...] = (acc / denom).astype(o_ref.dtype)


def _moba_decode_attention(q, new_rows, cache, layer, sel_pages_flat, sel_blocks_flat, scale, past_len):
    b, r, w = q.shape
    n_sel = N_HEADS * MOBA_TOPK * (MOBA_BLOCK // PAGE_SIZE)
    specs = [pl.BlockSpec((None, r, w), lambda a, pg, bk: (a, 0, 0)),
             pl.BlockSpec((None, 1, w), lambda a, pg, bk: (a, 0, 0))]
    specs += [pl.BlockSpec((None, None, PAGE_SIZE, w),
                           functools.partial(lambda a, pg, bk, u: (layer, pg[a * n_sel + u], 0, 0), u=u))
              for u in range(n_sel)]
    return pl.pallas_call(
        functools.partial(_moba_decode_body, scale=scale, past_len=past_len),
        grid_spec=pltpu.PrefetchScalarGridSpec(
            num_scalar_prefetch=2, grid=(b,), in_specs=specs,
            out_specs=pl.BlockSpec((None, r, w), lambda a, pg, bk: (a, 0, 0))),
        out_shape=jax.ShapeDtypeStruct((b, r, w), BF16),
        compiler_params=_cparams("arbitrary"),
        name="decode_moba",
    )(sel_pages_flat, sel_blocks_flat, q, new_rows, *([cache] * n_sel))


def _rope_tables(pos):
    half = MLA_ROPE_DIM // 2
    freqs = ROPE_THETA ** (-jnp.arange(half, dtype=F32) / half)
    ang = pos.astype(F32)[:, None] * freqs
    return jnp.cos(ang), jnp.sin(ang)


def _rope(x, cos, sin):
    half = x.shape[-1] // 2
    shape = (1, cos.shape[0]) + (1,) * (x.ndim - 3) + (half,)
    cos, sin = cos.reshape(shape), sin.reshape(shape)
    x1, x2 = x[..., :half], x[..., half:]
    return jnp.concatenate([x1 * cos - x2 * sin, x1 * sin + x2 * cos], -1)


def _heads_first(x, b, t, n, d):
    return x.reshape(b, t, n, d).transpose(2, 0, 1, 3)


def _prep_weights(l, w_in, mla_w_q_up, mla_w_uk, mla_w_uv, w_branch, w_gate, b_gate, w_out, mem_w_q, mem_w_k,
                  mem_w_v, mem_w_o, moe_w1, moe_w3, moe_w2):
    pad = (-IN_COLS) % LANES
    return dict(
        w_in=jnp.pad(w_in[l], ((0, 0), (0, pad))).astype(BF16),
        w_q_up=mla_w_q_up[l].reshape(MLA_Q_LORA, -1).astype(BF16),
        w_uk=mla_w_uk[l].transpose(1, 2, 0).astype(BF16),
        w_uv=mla_w_uv[l].transpose(1, 0, 2).astype(BF16),
        w_branch=w_branch[l].astype(BF16),
        w_gate=w_gate[l].astype(BF16),
        b_gate=b_gate[l].reshape(1, N_BRANCHES * D_MODEL),
        w_out=w_out[l].astype(BF16),
        mem_w_q=mem_w_q[l].astype(BF16), mem_w_k=mem_w_k[l].astype(BF16), mem_w_v=mem_w_v[l].astype(BF16),
        mem_w_o=mem_w_o[l].astype(BF16),
        moe_w1=moe_w1[l].astype(BF16), moe_w3=moe_w3[l].astype(BF16), moe_w2=moe_w2[l].astype(BF16),
    )


def _project_inputs(x_bf16, b, t, pos, wl, mla_g_q, mla_g_kv, fox_b_f):
    n = b * t
    h = _matmul(x_bf16, wl["w_in"])
    o = 0
    c_q, c_kv, k_r = h[:, :MLA_Q_LORA], h[:, MLA_Q_LORA:MLA_Q_LORA + MLA_KV_LORA], h[:, MLA_Q_LORA + MLA_KV_LORA:MLA_COLS]
    o += MLA_COLS
    q_b, rows_b = h[:, o:o + BRANCH_WIDTH], h[:, o + BRANCH_WIDTH:o + MOBA_COLS]
    o += MOBA_COLS
    q_c, rows_c = h[:, o:o + BRANCH_WIDTH], h[:, o + BRANCH_WIDTH:o + DIFF_COLS]
    o += DIFF_COLS
    q_d, kv_d, f_d = h[:, o:o + BRANCH_WIDTH], h[:, o + BRANCH_WIDTH:o + BRANCH_WIDTH + 2 * HEAD_DIM], h[:, o + BRANCH_WIDTH + 2 * HEAD_DIM:o + FOX_COLS]

    cos, sin = _rope_tables(pos)
    q = _matmul(c_q, wl["w_q_up"], rms_gain=mla_g_q).reshape(n, N_HEADS, MLA_NOPE_DIM + MLA_ROPE_DIM)
    q_nope = q[..., :MLA_NOPE_DIM].transpose(1, 0, 2).astype(BF16)
    q_lat = _head_matmul(q_nope, wl["w_uk"], BF16)
    q_rope = _rope(q[..., MLA_NOPE_DIM:].reshape(b, t, N_HEADS, MLA_ROPE_DIM), cos, sin)
    q_rope = q_rope.reshape(n, N_HEADS, MLA_ROPE_DIM).transpose(1, 0, 2).astype(BF16)
    q_a = jnp.concatenate([q_lat, q_rope], -1)
    c_kv_n = c_kv * lax.rsqrt(jnp.mean(c_kv * c_kv, -1, keepdims=True) + RMS_EPS) * mla_g_kv
    rows_a = jnp.concatenate([c_kv_n, _rope(k_r.reshape(b, t, MLA_ROPE_DIM), cos, sin).reshape(n, MLA_ROPE_DIM)], -1)
    log_f = jax.nn.log_sigmoid(f_d + fox_b_f)
    rows_d = jnp.concatenate([kv_d, log_f], -1)
    return dict(q_a=q_a, rows_a=rows_a, q_b=q_b, rows_b=rows_b, q_c=q_c, rows_c=rows_c, q_d=q_d, rows_d=rows_d,
                log_f=log_f)


def _diff_lambda(lam_p, lam_init):
    lam_p = lam_p.astype(F32)
    return jnp.exp(jnp.sum(lam_p[0] * lam_p[1])) - jnp.exp(jnp.sum(lam_p[2] * lam_p[3])) + lam_init


def _finish_layer(x, x_bf16, branches, mem_k, mem_v, mem_layer, b, t, wl, ln_g, ln_b, router_w, router_b, alpha):
    merged = _gated_merge(x_bf16, wl["w_gate"], wl["b_gate"], branches, wl["w_branch"])
    x, x_bf16 = _matmul_residual_ln(merged, wl["w_out"], x, ln_g[0], ln_b[0], alpha)
    q_m = _matmul(x_bf16, wl["mem_w_q"], out_dtype=BF16).reshape(b, t, MEM_WIDTH)
    o_m = _mem_attention(q_m, mem_k, mem_v, mem_layer).reshape(b * t, MEM_WIDTH)
    x, x_bf16 = _matmul_residual_ln(o_m, wl["mem_w_o"], x, ln_g[1], ln_b[1], alpha)
    gate = _router(x_bf16, router_w, router_b)
    return _moe_residual_ln(x_bf16, gate, wl["moe_w1"], wl["moe_w3"], wl["moe_w2"], x, ln_g[2], ln_b[2], alpha)


def _prompt_mixers(pr, b, t, wl, lam, g_norm, lam_init):
    n = b * t
    k_a = pr["rows_a"].reshape(b, t, -1).astype(BF16)
    o_lat = _flash_attention("mla", pr["q_a"].reshape(N_HEADS, b, t, -1), k_a, k_a[..., :MLA_KV_LORA],
                             (MLA_NOPE_DIM + MLA_ROPE_DIM) ** -0.5)
    o_a = _head_matmul(o_lat.reshape(N_HEADS, n, MLA_KV_LORA), wl["w_uv"], BF16)
    rows_b = pr["rows_b"].reshape(b, t, -1)
    kmean = _moba_block_means(rows_b[..., :HEAD_DIM])
    kmean = jnp.pad(kmean, ((0, 0), (0, LANES - kmean.shape[1]), (0, 0))).astype(BF16)
    kv_b = rows_b.astype(BF16)
    o_b = _flash_attention("moba", _heads_first(pr["q_b"], b, t, N_HEADS, HEAD_DIM).astype(BF16),
                           kv_b[..., :HEAD_DIM], kv_b[..., HEAD_DIM:], HEAD_DIM ** -0.5, kmean=kmean)
    q_c = pr["q_c"].reshape(b, t, N_HEADS, 2, DIFF_QK_DIM).transpose(3, 2, 0, 1, 4)
    zero = jnp.zeros_like(q_c[0])
    q_c = jnp.concatenate([jnp.concatenate([q_c[0], zero], -1), jnp.concatenate([zero, q_c[1]], -1)], 0).astype(BF16)
    kv_c = pr["rows_c"].reshape(b, t, -1).astype(BF16)
    o_c = _flash_attention("diff", q_c, kv_c[..., :2 * DIFF_QK_DIM], kv_c[..., 2 * DIFF_QK_DIM:], DIFF_QK_DIM ** -0.5,
                           lam=lam, g_norm=g_norm.reshape(N_HEADS, 1, DIFF_V_DIM), lam_init=lam_init)
    c = _cumsum_rows(pr["log_f"].reshape(b, t, N_HEADS).transpose(0, 2, 1)).reshape(b, N_HEADS, 1, t)
    kv_d = pr["rows_d"].reshape(b, t, -1).astype(BF16)
    o_d = _flash_attention("fox", _heads_first(pr["q_d"], b, t, N_HEADS, HEAD_DIM).astype(BF16),
                           kv_d[..., :HEAD_DIM], kv_d[..., HEAD_DIM:2 * HEAD_DIM], HEAD_DIM ** -0.5, c=c)
    to_tokens = lambda o: o.reshape(N_HEADS, n, HEAD_DIM).transpose(1, 0, 2).reshape(n, BRANCH_WIDTH)
    return jnp.stack([to_tokens(o_a), to_tokens(o_b), to_tokens(o_c), to_tokens(o_d)])


def _sample_mixers(pr, b, wl, lam, g_norm, lam_init, layer, caches, pt_flat, n_pages):
    cache_mla, cache_moba, cache_diff, cache_fox = caches
    past_len = n_pages * PAGE_SIZE
    per_tok = lambda q: q.transpose(1, 0, 2)
    o_lat = _decode_attention("mla", per_tok(pr["q_a"]), pr["rows_a"].reshape(b, 1, -1), cache_mla, layer, pt_flat,
                              (MLA_NOPE_DIM + MLA_ROPE_DIM) ** -0.5)
    o_a = _head_matmul(o_lat.transpose(1, 0, 2), wl["w_uv"], BF16).transpose(1, 0, 2)
    q_b = pr["q_b"].reshape(b, N_HEADS, HEAD_DIM)
    q_b = jnp.concatenate([q_b, jnp.zeros_like(q_b)], -1).astype(BF16)
    n_blocks = past_len // MOBA_BLOCK
    kmean = _moba_block_means_paged(cache_moba, layer, pt_flat, b, n_pages)
    kmean = jnp.pad(kmean, ((0, 0), (0, LANES - n_blocks), (0, 0)))
    choice = _moba_pick(q_b, kmean, n_blocks)[:, :, :MOBA_TOPK]
    per_block = MOBA_BLOCK // PAGE_SIZE
    page_pos = (choice[..., None] * per_block + jnp.arange(per_block, dtype=jnp.int32)).reshape(b, -1)
    sel_pages = jnp.take_along_axis(pt_flat.reshape(b, n_pages), page_pos, axis=1)
    o_b = _moba_decode_attention(q_b, pr["rows_b"].reshape(b, 1, -1), cache_moba, layer, sel_pages.reshape(-1),
                                 choice.reshape(-1), HEAD_DIM ** -0.5, past_len)[..., HEAD_DIM:]
    q_c = pr["q_c"].reshape(b, N_HEADS, 2, DIFF_QK_DIM).transpose(0, 2, 1, 3)
    z1 = jnp.zeros((b, N_HEADS, DIFF_QK_DIM), F32)
    z2 = jnp.zeros((b, N_HEADS, DIFF_V_DIM), F32)
    q_c = jnp.concatenate([jnp.concatenate([q_c[:, 0], z1, z2], -1), jnp.concatenate([z1, q_c[:, 1], z2], -1)], 1)
    o_c = _decode_attention("diff", q_c.astype(BF16), pr["rows_c"].reshape(b, 1, -1), cache_diff, layer, pt_flat,
                            DIFF_QK_DIM ** -0.5, lam=lam, g_norm=g_norm.reshape(N_HEADS, DIFF_V_DIM),
                            lam_init=lam_init)
    q_d = pr["q_d"].reshape(b, N_HEADS, HEAD_DIM)
    q_d = jnp.concatenate([q_d, jnp.zeros((b, N_HEADS, HEAD_DIM + N_HEADS), F32)], -1).astype(BF16)
    o_d = _decode_attention("fox", q_d, pr["rows_d"].reshape(b, 1, -1), cache_fox, layer, pt_flat, HEAD_DIM ** -0.5,
                            logf_new=pr["log_f"].reshape(b, N_HEADS, 1))[..., HEAD_DIM:]
    flat = lambda o: o.reshape(b, BRANCH_WIDTH)
    return jnp.stack([flat(o_a), flat(o_b), flat(o_c), flat(o_d)])


def kernel(x_prompt, x_sample, cache_mla, cache_moba, cache_diff, cache_fox, cache_mem_k, cache_mem_v, page_table, mem_prompt, ln_in_g, ln_in_b, ln_g, ln_b, w_in, mla_g_q, mla_g_kv, mla_w_q_up, mla_w_uk, mla_w_uv, diff_lambda, diff_g_norm, fox_b_f, w_branch, w_gate, b_gate, w_out, mem_w_q, mem_w_k, mem_w_v, mem_w_o, router_w, router_b, moe_w1, moe_w3, moe_w2):
    bp, tp, d = x_prompt.shape
    bs, ts, _ = x_sample.shape
    assert ts == 1 and tp % ATTN_BLOCK == 0 and tp // MOBA_BLOCK >= MOBA_TOPK
    depth = w_in.shape[0]
    n_pages = page_table.shape[1]
    assert n_pages % DECODE_PAGES == 0 and (n_pages * PAGE_SIZE) % MOBA_BLOCK == 0
    past_len = n_pages * PAGE_SIZE
    alpha = (2 * depth) ** 0.25
    pos_p = jnp.arange(tp)
    pos_s = past_len + jnp.arange(ts)
    pt_flat = page_table.reshape(-1)
    n_mem = mem_prompt.shape[1]

    router_w_p = jnp.pad(router_w, ((0, 0), (0, LANES - N_EXPERTS))).astype(BF16)
    router_b_p = jnp.pad(router_b.astype(F32), (0, LANES - N_EXPERTS)).reshape(1, LANES)
    mem_prompt_bf16 = mem_prompt.reshape(bp * n_mem, d).astype(BF16)
    cache_mla_t = cache_mla.transpose(0, 1, 3, 2)
    cache_fox_t = cache_fox.transpose(0, 1, 3, 2)

    xp, xp_b = _layer_norm(x_prompt.reshape(bp * tp, d), ln_in_g, ln_in_b)
    xs, xs_b = _layer_norm(x_sample.reshape(bs * ts, d), ln_in_g, ln_in_b)

    outs_p, outs_s, mem_ks, mem_vs = [], [], [], []
    for l in range(depth):
        lam_init = 0.8 - 0.6 * math.exp(-0.3 * l)
        lam = _diff_lambda(diff_lambda[l], lam_init)
        wl = _prep_weights(l, w_in, mla_w_q_up, mla_w_uk, mla_w_uv, w_branch, w_gate, b_gate, w_out, mem_w_q,
                           mem_w_k, mem_w_v, mem_w_o, moe_w1, moe_w3, moe_w2)
        mk = _matmul(mem_prompt_bf16, wl["mem_w_k"]).reshape(bp, n_mem, MEM_WIDTH)
        mv = _matmul(mem_prompt_bf16, wl["mem_w_v"]).reshape(bp, n_mem, MEM_WIDTH)

        pr = _project_inputs(xp_b, bp, tp, pos_p, wl, mla_g_q[l], mla_g_kv[l], fox_b_f[l])
        branches = _prompt_mixers(pr, bp, tp, wl, lam, diff_g_norm[l], lam_init)
        mem_rows = (n_mem * MEM_HEADS, MEM_HEAD_DIM)
        xp, xp_b = _finish_layer(xp, xp_b, branches, mk.reshape(1, bp, *mem_rows), mv.reshape(1, bp, *mem_rows), 0,
                                 bp, tp, wl, ln_g[l], ln_b[l], router_w_p, router_b_p, alpha)
        outs_p.append(pr)

        ps = _project_inputs(xs_b, bs, ts, pos_s, wl, mla_g_q[l], mla_g_kv[l], fox_b_f[l])
        branches = _sample_mixers(ps, bs, wl, lam, diff_g_norm[l], lam_init, l,
                                  (cache_mla_t, cache_moba, cache_diff, cache_fox_t), pt_flat, n_pages)
        xs, xs_b = _finish_layer(xs, xs_b, branches, cache_mem_k.reshape(depth, bs, *mem_rows),
                                 cache_mem_v.reshape(depth, bs, *mem_rows), l, bs, ts, wl, ln_g[l], ln_b[l],
                                 router_w_p, router_b_p, alpha)
        outs_s.append(ps)
        mem_ks.append(mk.reshape(bp, n_mem, MEM_HEADS, MEM_HEAD_DIM))
        mem_vs.append(mv.reshape(bp, n_mem, MEM_HEADS, MEM_HEAD_DIM))

    rows = lambda outs, name, b, t: jnp.stack([o[name].reshape(b, t, -1) for o in outs])
    return (xp.reshape(bp, tp, d), xs.reshape(bs, ts, d),
            rows(outs_p, "rows_a", bp, tp), rows(outs_p, "rows_b", bp, tp),
            rows(outs_p, "rows_c", bp, tp), rows(outs_p, "rows_d", bp, tp),
            jnp.stack(mem_ks), jnp.stack(mem_vs),
            rows(outs_s, "rows_a", bs, ts), rows(outs_s, "rows_b", bs, ts),
            rows(outs_s, "rows_c", bs, ts), rows(outs_s, "rows_d", bs, ts))
```

```python
import functools
import math

import numpy as np
import jax
import jax.numpy as jnp
from jax import lax
from jax.experimental import pallas as pl
from jax.experimental.pallas import tpu as pltpu

F32 = jnp.float32
BF16 = jnp.bfloat16

D_MODEL = 2048
N_HEADS = 8
HEAD_DIM = 64
BRANCH_WIDTH = N_HEADS * HEAD_DIM
N_BRANCHES = 4
MLA_Q_LORA = 384
MLA_KV_LORA = 128
MLA_NOPE_DIM = HEAD_DIM
MLA_ROPE_DIM = 32
ROPE_THETA = 10000.0
MOBA_BLOCK = 256
MOBA_TOPK = 3
DIFF_QK_DIM = HEAD_DIM // 2
DIFF_V_DIM = HEAD_DIM
MEM_HEADS = 4
MEM_HEAD_DIM = 128
MEM_WIDTH = MEM_HEADS * MEM_HEAD_DIM
N_EXPERTS = 16
EXPERTS_PER_GROUP = 4
D_EXPERT = 512
PAGE_SIZE = 128
LN_EPS = 1e-5
RMS_EPS = 1e-6
NEG_INF = -1e30
BELOW_NEG_INF = -3e38

MLA_COLS = MLA_Q_LORA + MLA_KV_LORA + MLA_ROPE_DIM
MOBA_COLS = BRANCH_WIDTH + 2 * HEAD_DIM
DIFF_COLS = N_HEADS * 2 * DIFF_QK_DIM + 2 * DIFF_QK_DIM + DIFF_V_DIM
FOX_COLS = BRANCH_WIDTH + 2 * HEAD_DIM + N_HEADS
IN_COLS = MLA_COLS + MOBA_COLS + DIFF_COLS + FOX_COLS

LANES = 128
VMEM_LIMIT_BYTES = 56 * 1024 * 1024

ATTN_BLOCK = 512
LOG2E = 1.4426950408889634
DECODE_PAGES = 128

_NT = (((1,), (1,)), ((), ()))


def _cparams(*sem):
    return pltpu.CompilerParams(dimension_semantics=sem, vmem_limit_bytes=VMEM_LIMIT_BYTES)


def _tile(n, candidates):
    for c in candidates:
        if n % c == 0:
            return c
    return n


def _layer_norm_rows(y, g, b):
    mu = jnp.mean(y, -1, keepdims=True)
    yc = y - mu
    var = jnp.mean(yc * yc, -1, keepdims=True)
    return yc * lax.rsqrt(var + LN_EPS) * g + b


def _ln_body(x_ref, g_ref, b_ref, o_ref, ob_ref):
    y = _layer_norm_rows(x_ref[...], g_ref[...], b_ref[...])
    o_ref[...] = y
    ob_ref[...] = y.astype(BF16)


def _layer_norm(x, g, b):
    m, d = x.shape
    tm = _tile(m, (512, 256, 128))
    return pl.pallas_call(
        _ln_body,
        grid=(m // tm,),
        in_specs=[pl.BlockSpec((tm, d), lambda i: (i, 0)),
                  pl.BlockSpec((1, d), lambda i: (0, 0)),
                  pl.BlockSpec((1, d), lambda i: (0, 0))],
        out_specs=[pl.BlockSpec((tm, d), lambda i: (i, 0)),
                   pl.BlockSpec((tm, d), lambda i: (i, 0))],
        out_shape=[jax.ShapeDtypeStruct((m, d), F32), jax.ShapeDtypeStruct((m, d), BF16)],
        compiler_params=_cparams("parallel"),
        name="layer_norm",
    )(x, g.reshape(1, d), b.reshape(1, d))


def _mm_body(x_ref, w_ref, o_ref):
    o_ref[...] = jnp.dot(x_ref[...].astype(BF16), w_ref[...], preferred_element_type=F32).astype(o_ref.dtype)


def _mm_rms_body(x_ref, g_ref, w_ref, o_ref):
    x = x_ref[...]
    xn = x * lax.rsqrt(jnp.mean(x * x, -1, keepdims=True) + RMS_EPS) * g_ref[...]
    o_ref[...] = jnp.dot(xn.astype(BF16), w_ref[...], preferred_element_type=F32).astype(o_ref.dtype)


def _matmul(x, w, out_dtype=F32, rms_gain=None):
    m, k = x.shape
    n = w.shape[1]
    tm = _tile(m, (512, 256, 128))
    tn = _tile(n, (1024, 768, 640, 512, 384, 256, 128))
    x_spec = pl.BlockSpec((tm, k), lambda i, j: (i, 0))
    w_spec = pl.BlockSpec((k, tn), lambda i, j: (0, j))
    o_spec = pl.BlockSpec((tm, tn), lambda i, j: (i, j))
    if rms_gain is None:
        body, specs, args = _mm_body, [x_spec, w_spec], (x, w)
    else:
        body = _mm_rms_body
        specs = [x_spec, pl.BlockSpec((1, k), lambda i, j: (0, 0)), w_spec]
        args = (x, rms_gain.reshape(1, k), w)
    return pl.pallas_call(
        body, grid=(m // tm, n // tn), in_specs=specs, out_specs=o_spec,
        out_shape=jax.ShapeDtypeStruct((m, n), out_dtype),
        compiler_params=_cparams("parallel", "arbitrary"),
        name="matmul",
    )(*args)


def _bmm_body(x_ref, w_ref, o_ref):
    o_ref[...] = jnp.dot(x_ref[...], w_ref[...], preferred_element_type=F32).astype(o_ref.dtype)


def _head_matmul(x, w, out_dtype):
    h, m, k = x.shape
    n = w.shape[2]
    tm = _tile(m, (1024, 512, 256, 128))
    return pl.pallas_call(
        _bmm_body, grid=(h, m // tm),
        in_specs=[pl.BlockSpec((None, tm, k), lambda a, i: (a, i, 0)),
                  pl.BlockSpec((None, k, n), lambda a, i: (a, 0, 0))],
        out_specs=pl.BlockSpec((None, tm, n), lambda a, i: (a, i, 0)),
        out_shape=jax.ShapeDtypeStruct((h, m, n), out_dtype),
        compiler_params=_cparams("parallel", "arbitrary"),
        name="head_matmul",
    )(x, w)


def _mm_ln_body(x_ref, w_ref, r_ref, g_ref, b_ref, o_ref, ob_ref, *, alpha):
    y = alpha * r_ref[...] + jnp.dot(x_ref[...], w_ref[...], preferred_element_type=F32)
    y = _layer_norm_rows(y, g_ref[...], b_ref[...])
    o_ref[...] = y
    ob_ref[...] = y.astype(BF16)


def _matmul_residual_ln(x, w, resid, g, b, alpha):
    m, k = x.shape
    d = w.shape[1]
    tm = _tile(m, (256, 128))
    row = lambda i: (i, 0)
    fixed = lambda i: (0, 0)
    return pl.pallas_call(
        functools.partial(_mm_ln_body, alpha=alpha),
        grid=(m // tm,),
        in_specs=[pl.BlockSpec((tm, k), row), pl.BlockSpec((k, d), fixed), pl.BlockSpec((tm, d), row),
                  pl.BlockSpec((1, d), fixed), pl.BlockSpec((1, d), fixed)],
        out_specs=[pl.BlockSpec((tm, d), row), pl.BlockSpec((tm, d), row)],
        out_shape=[jax.ShapeDtypeStruct((m, d), F32), jax.ShapeDtypeStruct((m, d), BF16)],
        compiler_params=_cparams("parallel"),
        name="matmul_residual_ln",
    )(x, w, resid, g.reshape(1, d), b.reshape(1, d))


def _merge_body(x_ref, wg_ref, bg_ref, o_ref, wb_ref, out_ref, acc_ref):
    n = pl.program_id(2)

    @pl.when(n == 0)
    def _():
        acc_ref[...] = jnp.zeros_like(acc_ref)

    gate = jax.nn.sigmoid(jnp.dot(x_ref[...], wg_ref[...], preferred_element_type=F32) + bg_ref[...])
    proj = jnp.dot(o_ref[...], wb_ref[...], preferred_element_type=F32)
    acc_ref[...] += gate * proj

    @pl.when(n == N_BRANCHES - 1)
    def _():
        out_ref[...] = acc_ref[...].astype(out_ref.dtype)


def _gated_merge(x, w_gate, b_gate, branches, w_branch):
    m, d = x.shape
    c = branches.shape[2]
    tm = _tile(m, (1024, 512, 256, 128))
    tn = 512
    nj = d // tn
    return pl.pallas_call(
        _merge_body,
        grid=(m // tm, nj, N_BRANCHES),
        in_specs=[pl.BlockSpec((tm, d), lambda i, j, n: (i, 0)),
                  pl.BlockSpec((d, tn), lambda i, j, n: (0, n * nj + j)),
                  pl.BlockSpec((1, tn), lambda i, j, n: (0, n * nj + j)),
                  pl.BlockSpec((None, tm, c), lambda i, j, n: (n, i, 0)),
                  pl.BlockSpec((None, c, tn), lambda i, j, n: (n, 0, j))],
        out_specs=pl.BlockSpec((tm, tn), lambda i, j, n: (i, j)),
        out_shape=jax.ShapeDtypeStruct((m, d), BF16),
        scratch_shapes=[pltpu.VMEM((tm, tn), F32)],
        compiler_params=_cparams("parallel", "arbitrary", "arbitrary"),
        name="gated_merge",
    )(x, w_gate, b_gate, branches, w_branch)


def _mem_attn_body(q_ref, k_ref, v_ref, o_ref):
    scale = MEM_HEAD_DIM ** -0.5
    outs = []
    for h in range(MEM_HEADS):
        sl = slice(h * MEM_HEAD_DIM, (h + 1) * MEM_HEAD_DIM)
        n_mem = k_ref.shape[0] // MEM_HEADS
        k = k_ref[pl.ds(h, n_mem, stride=MEM_HEADS), :].astype(BF16)
        v = v_ref[pl.ds(h, n_mem, stride=MEM_HEADS), :].astype(BF16)
        s = lax.dot_general(q_ref[:, sl], k, _NT, preferred_element_type=F32) * scale
        m = jnp.max(s, -1, keepdims=True)
        e = jnp.exp(s - m)
        p = e / jnp.sum(e, -1, keepdims=True)
        outs.append(jnp.dot(p.astype(BF16), v, preferred_element_type=F32))
    o_ref[...] = jnp.concatenate(outs, axis=-1).astype(o_ref.dtype)


def _mem_attention(q, mem_k, mem_v, layer):
    b, t, w = q.shape
    rows, hd = mem_k.shape[2:]
    tq = _tile(t, (512, 256, 128))
    return pl.pallas_call(
        _mem_attn_body, grid=(b, t // tq),
        in_specs=[pl.BlockSpec((None, tq, w), lambda a, i: (a, i, 0)),
                  pl.BlockSpec((None, None, rows, hd), lambda a, i: (layer, a, 0, 0)),
                  pl.BlockSpec((None, None, rows, hd), lambda a, i: (layer, a, 0, 0))],
        out_specs=pl.BlockSpec((None, tq, w), lambda a, i: (a, i, 0)),
        out_shape=jax.ShapeDtypeStruct((b, t, w), BF16),
        compiler_params=_cparams("parallel", "arbitrary"),
        name="mem_attention",
    )(q, mem_k, mem_v)


def _router_body(x_ref, w_ref, b_ref, gate_ref):
    logits = jnp.dot(x_ref[...], w_ref[...], preferred_element_type=F32)
    scores = jax.nn.sigmoid(logits)
    sel = scores + b_ref[...]
    lane = lax.broadcasted_iota(jnp.int32, sel.shape, 1)
    in_grp = lane % EXPERTS_PER_GROUP
    grp = lane // EXPERTS_PER_GROUP
    real = lane < N_EXPERTS

    rank = jnp.zeros(sel.shape, jnp.int32)
    for d in range(1, EXPERTS_PER_GROUP):
        lower = pltpu.roll(sel, d, 1)
        upper = pltpu.roll(sel, LANES - d, 1)
        rank += jnp.where((in_grp >= d) & (lower >= sel), 1, 0)
        rank += jnp.where((in_grp + d < EXPERTS_PER_GROUP) & (upper > sel), 1, 0)
    top2 = rank < 2

    v = jnp.where(top2, sel, 0.0)
    gs = v
    for d in range(1, EXPERTS_PER_GROUP):
        gs += jnp.where(in_grp >= d, pltpu.roll(v, d, 1), 0.0)
        gs += jnp.where(in_grp + d < EXPERTS_PER_GROUP, pltpu.roll(v, LANES - d, 1), 0.0)

    n_groups = N_EXPERTS // EXPERTS_PER_GROUP
    beaten = jnp.zeros(sel.shape, jnp.int32)
    for d in range(1, n_groups):
        lower = pltpu.roll(gs, d * EXPERTS_PER_GROUP, 1)
        upper = pltpu.roll(gs, LANES - d * EXPERTS_PER_GROUP, 1)
        beaten += jnp.where((grp >= d) & (lower >= gs), 1, 0)
        beaten += jnp.where((grp + d < n_groups) & (upper > gs), 1, 0)
    chosen = top2 & (beaten == 0) & real

    w_sel = jnp.where(chosen, scores, 0.0)
    gate_ref[...] = w_sel / jnp.sum(w_sel, -1, keepdims=True)


def _router(x, router_w, router_b):
    m, d = x.shape
    tm = _tile(m, (512, 256, 128))
    return pl.pallas_call(
        _router_body, grid=(m // tm,),
        in_specs=[pl.BlockSpec((tm, d), lambda i: (i, 0)),
                  pl.BlockSpec((d, LANES), lambda i: (0, 0)),
                  pl.BlockSpec((1, LANES), lambda i: (0, 0))],
        out_specs=pl.BlockSpec((tm, LANES), lambda i: (i, 0)),
        out_shape=jax.ShapeDtypeStruct((m, LANES), F32),
        compiler_params=_cparams("parallel"),
        name="moe_router",
    )(x, router_w, router_b)


def _moe_body(x_ref, gate_ref, w1_ref, w3_ref, w2_ref, r_ref, g_ref, b_ref, o_ref, ob_ref, acc_ref, *, alpha):
    e = pl.program_id(1)

    @pl.when(e == 0)
    def _():
        acc_ref[...] = jnp.zeros_like(acc_ref)

    x = x_ref[...]
    h1 = jnp.dot(x, w1_ref[...], preferred_element_type=F32)
    h3 = jnp.dot(x, w3_ref[...], preferred_element_type=F32)
    gate = gate_ref[...]
    lane = lax.broadcasted_iota(jnp.int32, gate.shape, 1)
    ge = jnp.sum(jnp.where(lane == e, gate, 0.0), -1, keepdims=True)
    hdn = jax.nn.silu(h1) * h3 * ge
    acc_ref[...] += jnp.dot(hdn.astype(BF16), w2_ref[...], preferred_element_type=F32)

    @pl.when(e == N_EXPERTS - 1)
    def _():
        y = _layer_norm_rows(alpha * r_ref[...] + acc_ref[...], g_ref[...], b_ref[...])
        o_ref[...] = y
        ob_ref[...] = y.astype(BF16)


def _moe_residual_ln(x, gate, w1, w3, w2, resid, g, b, alpha):
    m, d = x.shape
    f = w1.shape[2]
    tm = _tile(m, (512, 256, 128))
    row = lambda i, e: (i, 0)
    fixed = lambda i, e: (0, 0)
    return pl.pallas_call(
        functools.partial(_moe_body, alpha=alpha),
        grid=(m // tm, N_EXPERTS),
        in_specs=[pl.BlockSpec((tm, d), row), pl.BlockSpec((tm, LANES), row),
                  pl.BlockSpec((None, d, f), lambda i, e: (e, 0, 0)),
                  pl.BlockSpec((None, d, f), lambda i, e: (e, 0, 0)),
                  pl.BlockSpec((None, f, d), lambda i, e: (e, 0, 0)),
                  pl.BlockSpec((tm, d), row), pl.BlockSpec((1, d), fixed), pl.BlockSpec((1, d), fixed)],
        out_specs=[pl.BlockSpec((tm, d), row), pl.BlockSpec((tm, d), row)],
        out_shape=[jax.ShapeDtypeStruct((m, d), F32), jax.ShapeDtypeStruct((m, d), BF16)],
        scratch_shapes=[pltpu.VMEM((tm, d), F32)],
        compiler_params=_cparams("parallel", "arbitrary"),
        name="moe_experts",
    )(x, gate, w1, w3, w2, resid, g.reshape(1, d), b.reshape(1, d))


def _lane_prefix_sum(x):
    lane = lax.broadcasted_iota(jnp.int32, x.shape, 1)
    shift = 1
    while shift < LANES:
        x = x + jnp.where(lane >= shift, pltpu.roll(x, shift, 1), 0.0)
        shift *= 2
    return x


def _cumsum_body(*refs, pieces, n_prefetch):
    x_refs, o_ref, carry_ref = refs[n_prefetch:-2], refs[-2], refs[-1]

    @pl.when(pl.program_id(1) == 0)
    def _():
        carry_ref[...] = jnp.zeros_like(carry_ref)

    carry = carry_ref[...]
    for p in range(pieces):
        x = x_refs[p][...] if len(x_refs) > 1 else x_refs[0][:, p * LANES:(p + 1) * LANES]
        c = _lane_prefix_sum(x) + carry
        o_ref[:, p * LANES:(p + 1) * LANES] = c
        carry = c[:, LANES - 1:LANES]
    carry_ref[...] = carry


def _cumsum_rows(x):
    b, r, l = x.shape
    pieces = _tile(l // LANES, (16, 8, 4, 2, 1))
    w = pieces * LANES
    return pl.pallas_call(
        functools.partial(_cumsum_body, pieces=pieces, n_prefetch=0),
        grid=(b, l // w),
        in_specs=[pl.BlockSpec((None, r, w), lambda a, j: (a, 0, j))],
        out_specs=pl.BlockSpec((None, r, w), lambda a, j: (a, 0, j)),
        out_shape=jax.ShapeDtypeStruct((b, r, l), F32),
        scratch_shapes=[pltpu.VMEM((r, 1), F32)],
        compiler_params=_cparams("parallel", "arbitrary"),
        name="cumsum_rows",
    )(x)


def _block_mean_body(k_ref, o_ref):
    o_ref[...] = jnp.sum(k_ref[...], 0, keepdims=True) * (1.0 / MOBA_BLOCK)


def _moba_block_means(k):
    b, l, d = k.shape
    nb = l // MOBA_BLOCK
    out = pl.pallas_call(
        _block_mean_body, grid=(b, nb),
        in_specs=[pl.BlockSpec((None, MOBA_BLOCK, d), lambda a, n: (a, n, 0))],
        out_specs=pl.BlockSpec((None, None, 1, d), lambda a, n: (a, n, 0, 0)),
        out_shape=jax.ShapeDtypeStruct((b, nb, 1, d), F32),
        compiler_params=_cparams("parallel", "arbitrary"),
        name="moba_block_means",
    )(k)
    return out.reshape(b, nb, d)


def _paged_block_mean_body(pt_ref, *refs):
    page_refs, o_ref = refs[:-1], refs[-1]
    per_block = MOBA_BLOCK // PAGE_SIZE
    for n in range(len(page_refs) // per_block):
        tot = jnp.sum(page_refs[n * per_block][...], 0, keepdims=True)
        for t in range(1, per_block):
            tot = tot + jnp.sum(page_refs[n * per_block + t][...], 0, keepdims=True)
        o_ref[n] = tot * (1.0 / MOBA_BLOCK)


def _moba_block_means_paged(cache, layer, page_table_flat, n_seq, n_pages):
    w = cache.shape[3]
    pp = DECODE_PAGES
    bps = pp * PAGE_SIZE // MOBA_BLOCK
    nb = n_pages * PAGE_SIZE // MOBA_BLOCK
    specs = [pl.BlockSpec((None, None, PAGE_SIZE, w),
                          functools.partial(lambda a, j, pt, p: (layer, pt[a * n_pages + j * pp + p], 0, 0), p=p))
             for p in range(pp)]
    out = pl.pallas_call(
        _paged_block_mean_body,
        grid_spec=pltpu.PrefetchScalarGridSpec(
            num_scalar_prefetch=1, grid=(n_seq, n_pages // pp),
            in_specs=specs,
            out_specs=pl.BlockSpec((None, bps, 1, w), lambda a, j, pt: (a, j, 0, 0))),
        out_shape=jax.ShapeDtypeStruct((n_seq, nb, 1, w), F32),
        compiler_params=_cparams("parallel", "arbitrary"),
        name="moba_block_means_paged",
    )(page_table_flat, *([cache] * pp))
    return out.reshape(n_seq, nb, w)


def _top_blocks(gate, limit):
    lane = lax.broadcasted_iota(jnp.int32, gate.shape, gate.ndim - 1)
    cur = jnp.where(lane < limit, gate, NEG_INF)
    picked = jnp.zeros(gate.shape, jnp.bool_)
    for _ in range(MOBA_TOPK):
        cur = jnp.where(picked, BELOW_NEG_INF, cur)
        best = jnp.max(cur, -1, keepdims=True)
        first = jnp.min(jnp.where(cur == best, lane, LANES), -1, keepdims=True)
        picked = picked | (lane == first)
    return picked & (lane < limit)


def _flash_body(qi_ref, kj_ref, *refs, mode, n_heads, blk, scale, lam_init):
    q_ref, k_ref, v_ref = refs[:3]
    refs = refs[3:]
    if mode == "fox":
        ck_ref = refs[0]
        refs = refs[1:]
    if mode in ("moba", "diff"):
        slope_ref = refs[0]
        refs = refs[1:]
    if mode == "moba":
        kmean_ref = refs[0]
        refs = refs[1:]
    if mode == "diff":
        lam_ref, gn_ref = refs[:2]
        refs = refs[2:]
    o_ref, m_ref, acc_ref = refs[:3]
    sel_ref = refs[3] if mode == "moba" else None

    t = pl.program_id(1)
    i = qi_ref[t]
    j = kj_ref[t]
    sub = blk // MOBA_BLOCK
    dk = q_ref.shape[-1]
    dv = o_ref.shape[-1]
    rows = n_heads * blk

    row = lax.broadcasted_iota(jnp.int32, (1, blk, blk), 1)
    col = lax.broadcasted_iota(jnp.int32, (1, blk, blk), 2)

    @pl.when(j == 0)
    def _():
        m_ref[...] = jnp.full_like(m_ref, NEG_INF)
        acc_ref[...] = jnp.zeros_like(acc_ref)
        if mode == "moba":
            own = (i * blk + lax.broadcasted_iota(jnp.int32, (1, blk, 1), 1)) // MOBA_BLOCK
            gate = lax.dot_general(q_ref[...].reshape(rows, dk), kmean_ref[...], _NT,
                                   preferred_element_type=F32).reshape(n_heads, blk, LANES)
            sel_ref[...] = jnp.where(_top_blocks(gate, own), 1.0, 0.0)

    def tile(diag):
        s = lax.dot_general(q_ref[...].reshape(rows, dk), k_ref[...], _NT, preferred_element_type=F32)
        s = s.reshape(n_heads, blk, blk) * (scale * LOG2E)
        if mode in ("moba", "diff"):
            key_off = ((j - i) * blk + lax.broadcasted_iota(jnp.int32, (1, 1, blk), 2)).astype(F32)
            s = s + (slope_ref[...] * LOG2E) * key_off
        if mode == "fox":
            s = s - ck_ref[...] * LOG2E
        if mode == "moba":
            sel = sel_ref[...]
            lane = lax.broadcasted_iota(jnp.int32, (1, 1, LANES), 2)
            picked = None
            for u in range(sub):
                on = jnp.sum(jnp.where(lane == j * sub + u, sel, 0.0), -1, keepdims=True) > 0.0
                if sub > 1:
                    on = on & (col // MOBA_BLOCK == u)
                picked = on if picked is None else picked | on
            if diag:
                own_blk = row // MOBA_BLOCK
                key_blk = col // MOBA_BLOCK
                valid = ((key_blk == own_blk) & (row >= col)) | ((key_blk < own_blk) & picked)
            else:
                valid = picked
            s = jnp.where(valid, s, NEG_INF)
        elif diag:
            s = jnp.where(row >= col, s, NEG_INF)
        m_prev = m_ref[...]
        m_new = jnp.maximum(m_prev, jnp.max(s, -1, keepdims=True))
        p = jnp.exp2(s - m_new)
        pv = jnp.dot(p.astype(BF16).reshape(rows, blk), v_ref[...], preferred_element_type=F32)
        acc_ref[...] = jnp.exp2(m_prev - m_new) * acc_ref[...] + pv.reshape(acc_ref.shape)
        m_ref[...] = m_new

    @pl.when(j < i)
    def _():
        tile(False)

    @pl.when(j == i)
    def _():
        tile(True)
        acc = acc_ref[...]
        out = acc[..., :dv] / acc[..., dv:dv + 1]
        if mode == "diff":
            a = out[:N_HEADS] - lam_ref[0, 0] * out[N_HEADS:]
            a = a * lax.rsqrt(jnp.mean(a * a, -1, keepdims=True) + RMS_EPS) * gn_ref[...]
            out = a * (1.0 - lam_init)
        o_ref[...] = out.astype(o_ref.dtype)


def _flash_attention(mode, q, k, v, scale, *, c=None, kmean=None, lam=None, g_norm=None, lam_init=0.0):
    nh, b, t, dk = q.shape
    q = q.astype(BF16)
    dv = v.shape[2]
    blk = ATTN_BLOCK
    nq = t // blk
    pairs = [(i, j) for i in range(nq) for j in range(i + 1)]
    qi = jnp.asarray(np.array([p[0] for p in pairs], np.int32))
    kj = jnp.asarray(np.array([p[1] for p in pairs], np.int32))
    h_out = N_HEADS if mode == "diff" else nh
    dva = -(-(dv + 1) // LANES) * LANES
    v = jnp.concatenate([v, jnp.ones((b, t, 1), BF16), jnp.zeros((b, t, dva - dv - 1), BF16)], -1)

    in_specs = [pl.BlockSpec((nh, None, blk, dk), lambda a, s, qi, kj: (0, a, qi[s], 0)),
                pl.BlockSpec((None, blk, dk), lambda a, s, qi, kj: (a, kj[s], 0)),
                pl.BlockSpec((None, blk, dva), lambda a, s, qi, kj: (a, kj[s], 0))]
    args = [q, k, v]
    scratch = [pltpu.VMEM((nh, blk, 1), F32), pltpu.VMEM((nh, blk, dva), F32)]
    if mode == "fox":
        in_specs.append(pl.BlockSpec((None, nh, 1, blk), lambda a, s, qi, kj: (a, 0, 0, kj[s])))
        args.append(c)
    if mode in ("moba", "diff"):
        in_specs.append(pl.BlockSpec((nh, 1, 1), lambda a, s, qi, kj: (0, 0, 0)))
        args.append(jnp.asarray(np.array([2.0 ** -(h % N_HEADS + 1) for h in range(nh)], np.float32).reshape(nh, 1, 1)))
    if mode == "moba":
        in_specs.append(pl.BlockSpec((None, LANES, dk), lambda a, s, qi, kj: (a, 0, 0)))
        args.append(kmean)
        scratch.append(pltpu.VMEM((nh, blk, LANES), F32))
    if mode == "diff":
        in_specs += [pl.BlockSpec(memory_space=pltpu.SMEM),
                     pl.BlockSpec((N_HEADS, 1, dv), lambda a, s, qi, kj: (0, 0, 0))]
        args += [lam.reshape(1, 1), g_norm]

    return pl.pallas_call(
        functools.partial(_flash_body, mode=mode, n_heads=nh, blk=blk, scale=scale, lam_init=lam_init),
        grid_spec=pltpu.PrefetchScalarGridSpec(
            num_scalar_prefetch=2, grid=(b, len(pairs)), in_specs=in_specs,
            out_specs=pl.BlockSpec((h_out, None, blk, dv), lambda a, s, qi, kj: (0, a, qi[s], 0)),
            scratch_shapes=scratch),
        out_shape=jax.ShapeDtypeStruct((h_out, b, t, dv), BF16),
        compiler_params=_cparams("parallel", "arbitrary"),
        name="flash_" + mode,
    )(qi, kj, *args)


def _page_prefix_sums(x):
    hi = x.astype(BF16)
    rest = x - hi.astype(F32)
    mid = rest.astype(BF16)
    low = (rest - mid.astype(F32)).astype(BF16)
    src = lax.broadcasted_iota(jnp.int32, (LANES, LANES), 0)
    dst = lax.broadcasted_iota(jnp.int32, (LANES, LANES), 1)
    tri = jnp.where(src <= dst, 1.0, 0.0).astype(BF16)
    dot = lambda a: jnp.dot(a, tri, preferred_element_type=F32)
    return dot(hi) + dot(mid) + dot(low)


def _decode_body(pt_ref, *refs, mode, scale, past_len, lam_init):
    pp = DECODE_PAGES
    keys_on_lanes = mode in ("mla", "fox")
    q_ref, new_ref = refs[:2]
    page_refs = refs[2:2 + pp]
    refs = refs[2 + pp:]
    if mode == "fox":
        logf_new_ref = refs[0]
        refs = refs[1:]
    if mode == "diff":
        lam_ref, gn_ref, slope_ref = refs[:3]
        refs = refs[3:]
    o_ref, m_ref, l_ref, acc_ref, keys_ref = refs[:5]

    j = pl.program_id(1)
    span = pp * PAGE_SIZE

    @pl.when(j == 0)
    def _():
        m_ref[...] = jnp.full_like(m_ref, NEG_INF)
        l_ref[...] = jnp.zeros_like(l_ref)
        acc_ref[...] = jnp.zeros_like(acc_ref)
        if mode == "fox":
            refs[5][...] = jnp.zeros_like(refs[5])

    q = q_ref[...]
    for p in range(pp):
        if keys_on_lanes:
            keys_ref[:, p * PAGE_SIZE:(p + 1) * PAGE_SIZE] = page_refs[p][...].astype(BF16)
        else:
            keys_ref[p * PAGE_SIZE:(p + 1) * PAGE_SIZE, :] = page_refs[p][...].astype(BF16)
    if keys_on_lanes:
        s = jnp.dot(q, keys_ref[...], preferred_element_type=F32) * scale
    else:
        s = lax.dot_general(q, keys_ref[...], _NT, preferred_element_type=F32) * scale
    if mode == "diff":
        dist = (past_len - j * span - lax.broadcasted_iota(jnp.int32, (1, span), 1)).astype(F32)
        s = s - slope_ref[...] * dist
    if mode == "fox":
        carry_ref = refs[5]
        logf = jnp.concatenate([page_refs[p][2 * HEAD_DIM:2 * HEAD_DIM + N_HEADS, :] for p in range(pp)], axis=0)
        within = _page_prefix_sums(logf)
        carry = carry_ref[...]
        pieces = []
        for p in range(pp):
            w = within[p * N_HEADS:(p + 1) * N_HEADS]
            pieces.append(w + carry)
            carry = carry + w[:, LANES - 1:LANES]
        carry_ref[...] = carry
        s = s - jnp.concatenate(pieces, axis=1)
    m_prev = m_ref[...]
    m_new = jnp.maximum(m_prev, jnp.max(s, -1, keepdims=True))
    alpha = jnp.exp(m_prev - m_new)
    p = jnp.exp(s - m_new)
    l_ref[...] = alpha * l_ref[...] + jnp.sum(p, -1, keepdims=True)
    pb = p.astype(BF16)
    if keys_on_lanes:
        pv = lax.dot_general(pb, keys_ref[:LANES, :], _NT, preferred_element_type=F32)
    else:
        pv = jnp.dot(pb, keys_ref[...], preferred_element_type=F32)
    acc_ref[...] = alpha * acc_ref[...] + pv
    m_ref[...] = m_new

    @pl.when(j == pl.num_programs(1) - 1)
    def _():
        new = new_ref[...].astype(BF16).astype(F32)
        s_new = jnp.sum(q.astype(F32) * new, -1, keepdims=True) * scale
        if mode == "fox":
            s_new = s_new - (refs[5][...] + logf_new_ref[...])
        m_old = m_ref[...]
        m_fin = jnp.maximum(m_old, s_new)
        a_old = jnp.exp(m_old - m_fin)
        p_new = jnp.exp(s_new - m_fin)
        l_fin = a_old * l_ref[...] + p_new
        acc_fin = a_old * acc_ref[...] + p_new.astype(BF16).astype(F32) * new[:, :LANES]
        out = acc_fin / l_fin
        if mode == "diff":
            a = out[:N_HEADS, DIFF_V_DIM:] - lam_ref[0, 0] * out[N_HEADS:, DIFF_V_DIM:]
            a = a * lax.rsqrt(jnp.mean(a * a, -1, keepdims=True) + RMS_EPS) * gn_ref[...]
            o_ref[...] = (a * (1.0 - lam_init)).astype(o_ref.dtype)
        else:
            o_ref[...] = out.astype(o_ref.dtype)


def _decode_attention(mode, q, new_rows, cache, layer, page_table_flat, scale, *, logf_new=None,
                      lam=None, g_norm=None, lam_init=0.0):
    b, n_rows, width = q.shape
    n_pages = page_table_flat.shape[0] // b
    pp = DECODE_PAGES
    past_len = n_pages * PAGE_SIZE
    page_block = (None, None, width, PAGE_SIZE) if mode in ("mla", "fox") else (None, None, PAGE_SIZE, width)
    in_specs = [pl.BlockSpec((None, n_rows, width), lambda a, j, pt: (a, 0, 0)),
                pl.BlockSpec((None, 1, width), lambda a, j, pt: (a, 0, 0))]
    in_specs += [pl.BlockSpec(page_block,
                              functools.partial(lambda a, j, pt, p: (layer, pt[a * n_pages + j * pp + p], 0, 0), p=p))
                 for p in range(pp)]
    args = [q, new_rows] + [cache] * pp
    out_rows, out_w = n_rows, LANES
    keys_shape = (width, pp * PAGE_SIZE) if mode in ("mla", "fox") else (pp * PAGE_SIZE, width)
    scratch = [pltpu.VMEM((n_rows, 1), F32), pltpu.VMEM((n_rows, 1), F32), pltpu.VMEM((n_rows, LANES), F32),
               pltpu.VMEM(keys_shape, BF16)]
    if mode == "fox":
        in_specs.append(pl.BlockSpec((None, n_rows, 1), lambda a, j, pt: (a, 0, 0)))
        args.append(logf_new)
        scratch.append(pltpu.VMEM((n_rows, 1), F32))
    if mode == "diff":
        in_specs += [pl.BlockSpec(memory_space=pltpu.SMEM),
                     pl.BlockSpec((N_HEADS, DIFF_V_DIM), lambda a, j, pt: (0, 0)),
                     pl.BlockSpec((n_rows, 1), lambda a, j, pt: (0, 0))]
        slopes = jnp.asarray(np.array([2.0 ** -(r % N_HEADS + 1) for r in range(n_rows)], np.float32).reshape(n_rows, 1))
        args += [lam.reshape(1, 1), g_norm, slopes]
        out_rows, out_w = N_HEADS, DIFF_V_DIM
    return pl.pallas_call(
        functools.partial(_decode_body, mode=mode, scale=scale, past_len=past_len, lam_init=lam_init),
        grid_spec=pltpu.PrefetchScalarGridSpec(
            num_scalar_prefetch=1, grid=(b, n_pages // pp), in_specs=in_specs,
            out_specs=pl.BlockSpec((None, out_rows, out_w), lambda a, j, pt: (a, 0, 0)),
            scratch_shapes=scratch),
        out_shape=jax.ShapeDtypeStruct((b, out_rows, out_w), BF16),
        compiler_params=_cparams("parallel", "arbitrary"),
        name="decode_" + mode,
    )(page_table_flat, *args)


def _moba_pick_body(q_ref, km_ref, o_ref, *, n_blocks):
    gate = jnp.concatenate([lax.dot_general(q_ref[g], km_ref[g].astype(BF16), _NT, preferred_element_type=F32)
                            for g in range(q_ref.shape[0])], axis=0)
    lane = lax.broadcasted_iota(jnp.int32, gate.shape, 1)
    cur = jnp.where(lane < n_blocks, gate, BELOW_NEG_INF)
    out = jnp.zeros(gate.shape, jnp.int32)
    for n in range(MOBA_TOPK):
        best = jnp.max(cur, -1, keepdims=True)
        first = jnp.min(jnp.where(cur == best, lane, LANES), -1, keepdims=True)
        out = jnp.where(lane == n, first, out)
        cur = jnp.where(lane == first, BELOW_NEG_INF, cur)
    o_ref[...] = out.reshape(o_ref.shape)


def _moba_pick(q, kmean_padded, n_blocks):
    b, r, w = q.shape
    g = _tile(b, (8, 4, 2, 1))
    return pl.pallas_call(
        functools.partial(_moba_pick_body, n_blocks=n_blocks), grid=(b // g,),
        in_specs=[pl.BlockSpec((g, r, w), lambda a: (a, 0, 0)),
                  pl.BlockSpec((g, LANES, w), lambda a: (a, 0, 0))],
        out_specs=pl.BlockSpec((g, r, LANES), lambda a: (a, 0, 0)),
        out_shape=jax.ShapeDtypeStruct((b, r, LANES), jnp.int32),
        compiler_params=_cparams("parallel"),
        name="moba_pick",
    )(q, kmean_padded)


def _moba_decode_body(pg_ref, blk_ref, q_ref, new_ref, *refs, scale, past_len):
    page_refs, o_ref = refs[:-1], refs[-1]
    a = pl.program_id(0)
    per_block = MOBA_BLOCK // PAGE_SIZE
    per_head = MOBA_TOPK * per_block
    n_sel = N_HEADS * per_head
    q = q_ref[...]
    row = lax.broadcasted_iota(jnp.int32, (N_HEADS, PAGE_SIZE), 0)
    lane = lax.broadcasted_iota(jnp.int32, (N_HEADS, PAGE_SIZE), 1)
    slope = jnp.exp2(-(row + 1).astype(F32))
    keys, scores = [], []
    for u in range(n_sel):
        h, rest = divmod(u, per_head)
        n, half = divmod(rest, per_block)
        kp = page_refs[u][...].astype(BF16)
        keys.append(kp)
        s = lax.dot_general(q, kp, _NT, preferred_element_type=F32) * scale
        first_pos = blk_ref[a * N_HEADS * MOBA_TOPK + h * MOBA_TOPK + n] * MOBA_BLOCK + half * PAGE_SIZE
        dist = (past_len - first_pos - lane).astype(F32)
        scores.append(jnp.where(row == h, s - slope * dist, NEG_INF))
    s = jnp.concatenate(scores, axis=1)
    new = new_ref[...].astype(BF16).astype(F32)
    s_new = jnp.sum(q.astype(F32) * new, -1, keepdims=True) * scale
    m = jnp.maximum(jnp.max(s, -1, keepdims=True), s_new)
    p = jnp.exp(s - m)
    p_new = jnp.exp(s_new - m)
    denom = jnp.sum(p, -1, keepdims=True) + p_new
    pb = p.astype(BF16)
    acc = p_new.astype(BF16).astype(F32) * new
    for u in range(n_sel):
        acc = acc + jnp.dot(pb[:, u * PAGE_SIZE:(u + 1) * PAGE_SIZE], keys[u], preferred_element_type=F32)
    o_ref[...] = (acc / denom).astype(o_ref.dtype)


def _moba_decode_attention(q, new_rows, cache, layer, sel_pages_flat, sel_blocks_flat, scale, past_len):
    b, r, w = q.shape
    n_sel = N_HEADS * MOBA_TOPK * (MOBA_BLOCK // PAGE_SIZE)
    specs = [pl.BlockSpec((None, r, w), lambda a, pg, bk: (a, 0, 0)),
             pl.BlockSpec((None, 1, w), lambda a, pg, bk: (a, 0, 0))]
    specs += [pl.BlockSpec((None, None, PAGE_SIZE, w),
                           functools.partial(lambda a, pg, bk, u: (layer, pg[a * n_sel + u], 0, 0), u=u))
              for u in range(n_sel)]
    return pl.pallas_call(
        functools.partial(_moba_decode_body, scale=scale, past_len=past_len),
        grid_spec=pltpu.PrefetchScalarGridSpec(
            num_scalar_prefetch=2, grid=(b,), in_specs=specs,
            out_specs=pl.BlockSpec((None, r, w), lambda a, pg, bk: (a, 0, 0))),
        out_shape=jax.ShapeDtypeStruct((b, r, w), BF16),
        compiler_params=_cparams("arbitrary"),
        name="decode_moba",
    )(sel_pages_flat, sel_blocks_flat, q, new_rows, *([cache] * n_sel))


def _rope_tables(pos):
    half = MLA_ROPE_DIM // 2
    freqs = ROPE_THETA ** (-jnp.arange(half, dtype=F32) / half)
    ang = pos.astype(F32)[:, None] * freqs
    return jnp.cos(ang), jnp.sin(ang)


def _rope(x, cos, sin):
    half = x.shape[-1] // 2
    shape = (1, cos.shape[0]) + (1,) * (x.ndim - 3) + (half,)
    cos, sin = cos.reshape(shape), sin.reshape(shape)
    x1, x2 = x[..., :half], x[..., half:]
    return jnp.concatenate([x1 * cos - x2 * sin, x1 * sin + x2 * cos], -1)


def _heads_first(x, b, t, n, d):
    return x.reshape(b, t, n, d).transpose(2, 0, 1, 3)


def _prep_weights(l, w_in, mla_w_q_up, mla_w_uk, mla_w_uv, w_branch, w_gate, b_gate, w_out, mem_w_q, mem_w_k,
                  mem_w_v, mem_w_o, moe_w1, moe_w3, moe_w2):
    pad = (-IN_COLS) % LANES
    return dict(
        w_in=jnp.pad(w_in[l], ((0, 0), (0, pad))).astype(BF16),
        w_q_up=mla_w_q_up[l].reshape(MLA_Q_LORA, -1).astype(BF16),
        w_uk=mla_w_uk[l].transpose(1, 2, 0).astype(BF16),
        w_uv=mla_w_uv[l].transpose(1, 0, 2).astype(BF16),
        w_branch=w_branch[l].astype(BF16),
        w_gate=w_gate[l].astype(BF16),
        b_gate=b_gate[l].reshape(1, N_BRANCHES * D_MODEL),
        w_out=w_out[l].astype(BF16),
        mem_w_q=mem_w_q[l].astype(BF16), mem_w_k=mem_w_k[l].astype(BF16), mem_w_v=mem_w_v[l].astype(BF16),
        mem_w_o=mem_w_o[l].astype(BF16),
        moe_w1=moe_w1[l].astype(BF16), moe_w3=moe_w3[l].astype(BF16), moe_w2=moe_w2[l].astype(BF16),
    )


def _project_inputs(x_bf16, b, t, pos, wl, mla_g_q, mla_g_kv, fox_b_f):
    n = b * t
    h = _matmul(x_bf16, wl["w_in"])
    o = 0
    c_q, c_kv, k_r = h[:, :MLA_Q_LORA], h[:, MLA_Q_LORA:MLA_Q_LORA + MLA_KV_LORA], h[:, MLA_Q_LORA + MLA_KV_LORA:MLA_COLS]
    o += MLA_COLS
    q_b, rows_b = h[:, o:o + BRANCH_WIDTH], h[:, o + BRANCH_WIDTH:o + MOBA_COLS]
    o += MOBA_COLS
    q_c, rows_c = h[:, o:o + BRANCH_WIDTH], h[:, o + BRANCH_WIDTH:o + DIFF_COLS]
    o += DIFF_COLS
    q_d, kv_d, f_d = h[:, o:o + BRANCH_WIDTH], h[:, o + BRANCH_WIDTH:o + BRANCH_WIDTH + 2 * HEAD_DIM], h[:, o + BRANCH_WIDTH + 2 * HEAD_DIM:o + FOX_COLS]

    cos, sin = _rope_tables(pos)
    q = _matmul(c_q, wl["w_q_up"], rms_gain=mla_g_q).reshape(n, N_HEADS, MLA_NOPE_DIM + MLA_ROPE_DIM)
    q_nope = q[..., :MLA_NOPE_DIM].transpose(1, 0, 2).astype(BF16)
    q_lat = _head_matmul(q_nope, wl["w_uk"], F32)
    q_rope = _rope(q[..., MLA_NOPE_DIM:].reshape(b, t, N_HEADS, MLA_ROPE_DIM), cos, sin)
    q_rope = q_rope.reshape(n, N_HEADS, MLA_ROPE_DIM).transpose(1, 0, 2)
    q_a = jnp.concatenate([q_lat, q_rope], -1)
    c_kv_n = c_kv * lax.rsqrt(jnp.mean(c_kv * c_kv, -1, keepdims=True) + RMS_EPS) * mla_g_kv
    rows_a = jnp.concatenate([c_kv_n, _rope(k_r.reshape(b, t, MLA_ROPE_DIM), cos, sin).reshape(n, MLA_ROPE_DIM)], -1)
    log_f = jax.nn.log_sigmoid(f_d + fox_b_f)
    rows_d = jnp.concatenate([kv_d, log_f], -1)
    return dict(q_a=q_a, rows_a=rows_a, q_b=q_b, rows_b=rows_b, q_c=q_c, rows_c=rows_c, q_d=q_d, rows_d=rows_d,
                log_f=log_f)


def _diff_lambda(lam_p, lam_init):
    lam_p = lam_p.astype(F32)
    return jnp.exp(jnp.sum(lam_p[0] * lam_p[1])) - jnp.exp(jnp.sum(lam_p[2] * lam_p[3])) + lam_init


def _finish_layer(x, x_bf16, branches, mem_k, mem_v, mem_layer, b, t, wl, ln_g, ln_b, router_w, router_b, alpha):
    merged = _gated_merge(x_bf16, wl["w_gate"], wl["b_gate"], branches, wl["w_branch"])
    x, x_bf16 = _matmul_residual_ln(merged, wl["w_out"], x, ln_g[0], ln_b[0], alpha)
    q_m = _matmul(x_bf16, wl["mem_w_q"], out_dtype=BF16).reshape(b, t, MEM_WIDTH)
    o_m = _mem_attention(q_m, mem_k, mem_v, mem_layer).reshape(b * t, MEM_WIDTH)
    x, x_bf16 = _matmul_residual_ln(o_m, wl["mem_w_o"], x, ln_g[1], ln_b[1], alpha)
    gate = _router(x_bf16, router_w, router_b)
    return _moe_residual_ln(x_bf16, gate, wl["moe_w1"], wl["moe_w3"], wl["moe_w2"], x, ln_g[2], ln_b[2], alpha)


def _prompt_mixers(pr, b, t, wl, lam, g_norm, lam_init):
    n = b * t
    k_a = pr["rows_a"].reshape(b, t, -1).astype(BF16)
    o_lat = _flash_attention("mla", pr["q_a"].reshape(N_HEADS, b, t, -1), k_a, k_a[..., :MLA_KV_LORA],
                             (MLA_NOPE_DIM + MLA_ROPE_DIM) ** -0.5)
    o_a = _head_matmul(o_lat.reshape(N_HEADS, n, MLA_KV_LORA), wl["w_uv"], BF16)
    rows_b = pr["rows_b"].reshape(b, t, -1)
    kmean = _moba_block_means(rows_b[..., :HEAD_DIM])
    kmean = jnp.pad(kmean, ((0, 0), (0, LANES - kmean.shape[1]), (0, 0))).astype(BF16)
    kv_b = rows_b.astype(BF16)
    o_b = _flash_attention("moba", _heads_first(pr["q_b"], b, t, N_HEADS, HEAD_DIM),
                           kv_b[..., :HEAD_DIM], kv_b[..., HEAD_DIM:], HEAD_DIM ** -0.5, kmean=kmean)
    q_c = pr["q_c"].reshape(b, t, N_HEADS, 2, DIFF_QK_DIM).transpose(3, 2, 0, 1, 4)
    zero = jnp.zeros_like(q_c[0])
    q_c = jnp.concatenate([jnp.concatenate([q_c[0], zero], -1), jnp.concatenate([zero, q_c[1]], -1)], 0)
    kv_c = pr["rows_c"].reshape(b, t, -1).astype(BF16)
    o_c = _flash_attention("diff", q_c, kv_c[..., :2 * DIFF_QK_DIM], kv_c[..., 2 * DIFF_QK_DIM:], DIFF_QK_DIM ** -0.5,
                           lam=lam, g_norm=g_norm.reshape(N_HEADS, 1, DIFF_V_DIM), lam_init=lam_init)
    c = _cumsum_rows(pr["log_f"].reshape(b, t, N_HEADS).transpose(0, 2, 1)).reshape(b, N_HEADS, 1, t)
    kv_d = pr["rows_d"].reshape(b, t, -1).astype(BF16)
    o_d = _flash_attention("fox", _heads_first(pr["q_d"], b, t, N_HEADS, HEAD_DIM),
                           kv_d[..., :HEAD_DIM], kv_d[..., HEAD_DIM:2 * HEAD_DIM], HEAD_DIM ** -0.5, c=c)
    to_tokens = lambda o: o.reshape(N_HEADS, n, HEAD_DIM).transpose(1, 0, 2).reshape(n, BRANCH_WIDTH)
    return jnp.stack([to_tokens(o_a), to_tokens(o_b), to_tokens(o_c), to_tokens(o_d)])


def _sample_mixers(pr, b, wl, lam, g_norm, lam_init, layer, caches, pt_flat, n_pages):
    cache_mla, cache_moba, cache_diff, cache_fox = caches
    past_len = n_pages * PAGE_SIZE
    per_tok = lambda q: q.transpose(1, 0, 2)
    o_lat = _decode_attention("mla", per_tok(pr["q_a"]).astype(BF16), pr["rows_a"].reshape(b, 1, -1), cache_mla, layer, pt_flat,
                              (MLA_NOPE_DIM + MLA_ROPE_DIM) ** -0.5)
    o_a = _head_matmul(o_lat.transpose(1, 0, 2), wl["w_uv"], BF16).transpose(1, 0, 2)
    q_b = pr["q_b"].reshape(b, N_HEADS, HEAD_DIM)
    q_b = jnp.concatenate([q_b, jnp.zeros_like(q_b)], -1).astype(BF16)
    n_blocks = past_len // MOBA_BLOCK
    kmean = _moba_block_means_paged(cache_moba, layer, pt_flat, b, n_pages)
    kmean = jnp.pad(kmean, ((0, 0), (0, LANES - n_blocks), (0, 0)))
    choice = _moba_pick(q_b, kmean, n_blocks)[:, :, :MOBA_TOPK]
    per_block = MOBA_BLOCK // PAGE_SIZE
    page_pos = (choice[..., None] * per_block + jnp.arange(per_block, dtype=jnp.int32)).reshape(b, -1)
    sel_pages = jnp.take_along_axis(pt_flat.reshape(b, n_pages), page_pos, axis=1)
    o_b = _moba_decode_attention(q_b, pr["rows_b"].reshape(b, 1, -1), cache_moba, layer, sel_pages.reshape(-1),
                                 choice.reshape(-1), HEAD_DIM ** -0.5, past_len)[..., HEAD_DIM:]
    q_c = pr["q_c"].reshape(b, N_HEADS, 2, DIFF_QK_DIM).transpose(0, 2, 1, 3)
    z1 = jnp.zeros((b, N_HEADS, DIFF_QK_DIM), F32)
    z2 = jnp.zeros((b, N_HEADS, DIFF_V_DIM), F32)
    q_c = jnp.concatenate([jnp.concatenate([q_c[:, 0], z1, z2], -1), jnp.concatenate([z1, q_c[:, 1], z2], -1)], 1)
    o_c = _decode_attention("diff", q_c.astype(BF16), pr["rows_c"].reshape(b, 1, -1), cache_diff, layer, pt_flat,
                            DIFF_QK_DIM ** -0.5, lam=lam, g_norm=g_norm.reshape(N_HEADS, DIFF_V_DIM),
                            lam_init=lam_init)
    q_d = pr["q_d"].reshape(b, N_HEADS, HEAD_DIM)
    q_d = jnp.concatenate([q_d, jnp.zeros((b, N_HEADS, HEAD_DIM + N_HEADS), F32)], -1).astype(BF16)
    o_d = _decode_attention("fox", q_d, pr["rows_d"].reshape(b, 1, -1), cache_fox, layer, pt_flat, HEAD_DIM ** -0.5,
                            logf_new=pr["log_f"].reshape(b, N_HEADS, 1))[..., HEAD_DIM:]
    flat = lambda o: o.reshape(b, BRANCH_WIDTH)
    return jnp.stack([flat(o_a), flat(o_b), flat(o_c), flat(o_d)])


def kernel(x_prompt, x_sample, cache_mla, cache_moba, cache_diff, cache_fox, cache_mem_k, cache_mem_v, page_table, mem_prompt, ln_in_g, ln_in_b, ln_g, ln_b, w_in, mla_g_q, mla_g_kv, mla_w_q_up, mla_w_uk, mla_w_uv, diff_lambda, diff_g_norm, fox_b_f, w_branch, w_gate, b_gate, w_out, mem_w_q, mem_w_k, mem_w_v, mem_w_o, router_w, router_b, moe_w1, moe_w3, moe_w2):
    bp, tp, d = x_prompt.shape
    bs, ts, _ = x_sample.shape
    assert ts == 1 and tp % ATTN_BLOCK == 0 and tp // MOBA_BLOCK >= MOBA_TOPK
    depth = w_in.shape[0]
    n_pages = page_table.shape[1]
    assert n_pages % DECODE_PAGES == 0 and (n_pages * PAGE_SIZE) % MOBA_BLOCK == 0
    past_len = n_pages * PAGE_SIZE
    alpha = (2 * depth) ** 0.25
    pos_p = jnp.arange(tp)
    pos_s = past_len + jnp.arange(ts)
    pt_flat = page_table.reshape(-1)
    n_mem = mem_prompt.shape[1]

    router_w_p = jnp.pad(router_w, ((0, 0), (0, LANES - N_EXPERTS))).astype(BF16)
    router_b_p = jnp.pad(router_b.astype(F32), (0, LANES - N_EXPERTS)).reshape(1, LANES)
    mem_prompt_bf16 = mem_prompt.reshape(bp * n_mem, d).astype(BF16)
    cache_mla_t = cache_mla.transpose(0, 1, 3, 2)
    cache_fox_t = cache_fox.transpose(0, 1, 3, 2)

    xp, xp_b = _layer_norm(x_prompt.reshape(bp * tp, d), ln_in_g, ln_in_b)
    xs, xs_b = _layer_norm(x_sample.reshape(bs * ts, d), ln_in_g, ln_in_b)

    outs_p, outs_s, mem_ks, mem_vs = [], [], [], []
    for l in range(depth):
        lam_init = 0.8 - 0.6 * math.exp(-0.3 * l)
        lam = _diff_lambda(diff_lambda[l], lam_init)
        wl = _prep_weights(l, w_in, mla_w_q_up, mla_w_uk, mla_w_uv, w_branch, w_gate, b_gate, w_out, mem_w_q,
                           mem_w_k, mem_w_v, mem_w_o, moe_w1, moe_w3, moe_w2)
        mk = _matmul(mem_prompt_bf16, wl["mem_w_k"]).reshape(bp, n_mem, MEM_WIDTH)
        mv = _matmul(mem_prompt_bf16, wl["mem_w_v"]).reshape(bp, n_mem, MEM_WIDTH)

        pr = _project_inputs(xp_b, bp, tp, pos_p, wl, mla_g_q[l], mla_g_kv[l], fox_b_f[l])
        branches = _prompt_mixers(pr, bp, tp, wl, lam, diff_g_norm[l], lam_init)
        mem_rows = (n_mem * MEM_HEADS, MEM_HEAD_DIM)
        xp, xp_b = _finish_layer(xp, xp_b, branches, mk.reshape(1, bp, *mem_rows), mv.reshape(1, bp, *mem_rows), 0,
                                 bp, tp, wl, ln_g[l], ln_b[l], router_w_p, router_b_p, alpha)
        outs_p.append(pr)

        ps = _project_inputs(xs_b, bs, ts, pos_s, wl, mla_g_q[l], mla_g_kv[l], fox_b_f[l])
        branches = _sample_mixers(ps, bs, wl, lam, diff_g_norm[l], lam_init, l,
                                  (cache_mla_t, cache_moba, cache_diff, cache_fox_t), pt_flat, n_pages)
        xs, xs_b = _finish_layer(xs, xs_b, branches, cache_mem_k.reshape(depth, bs, *mem_rows),
                                 cache_mem_v.reshape(depth, bs, *mem_rows), l, bs, ts, wl, ln_g[l], ln_b[l],
                                 router_w_p, router_b_p, alpha)
        outs_s.append(ps)
        mem_ks.append(mk.reshape(bp, n_mem, MEM_HEADS, MEM_HEAD_DIM))
        mem_vs.append(mv.reshape(bp, n_mem, MEM_HEADS, MEM_HEAD_DIM))

    rows = lambda outs, name, b, t: jnp.stack([o[name].reshape(b, t, -1) for o in outs])
    return (xp.reshape(bp, tp, d), xs.reshape(bs, ts, d),
            rows(outs_p, "rows_a", bp, tp), rows(outs_p, "rows_b", bp, tp),
            rows(outs_p, "rows_c", bp, tp), rows(outs_p, "rows_d", bp, tp),
            jnp.stack(mem_ks), jnp.stack(mem_vs),
            rows(outs_s, "rows_a", bs, ts), rows(outs_s, "rows_b", bs, ts),
            rows(outs_s, "rows_c", bs, ts), rows(outs_s, "rows_d", bs, ts))
```
